```python
import math
import jax, jax.numpy as jnp
from jax import lax
import numpy as np

D_MODEL = 4096
BATCH = 1
SEQ = 8192
DEPTH = 1
DEC_BATCH = 32
DEC_SEQ = 1
PAST_LEN = 8192
PAGE_SIZE = 128

RET_HEADS = 8
RET_DK = 256
RET_DV = 512
RET_CHUNK = 128
ROPE_BASE = 10000.0
ATT_HEADS = 16
ATT_DH = 128
IDX_HEADS = 8
IDX_DIM = 128
TOPK_MAX = 256
Q_BLOCK = 128
D_FF = ((8 * D_MODEL // 3 + 255) // 256) * 256
EPS = 1e-6

RET_QK = RET_HEADS * RET_DK
RET_V = RET_HEADS * RET_DV
ATT_W = ATT_HEADS * ATT_DH
IDX_Q = IDX_HEADS * IDX_DIM
SPLITS = (RET_QK, RET_QK, RET_V, RET_V, ATT_W, ATT_W, ATT_W, IDX_Q, IDX_DIM, IDX_HEADS, D_MODEL, D_MODEL)
IN_COLS = sum(SPLITS)

kernel_name = 'hybrid_retention_dsa_decode_step'


def rmsnorm(x, g):
    xf = x.astype(jnp.float32)
    y = xf * lax.rsqrt(jnp.mean(xf * xf, axis=-1, keepdims=True) + EPS)
    return (y * g.astype(jnp.float32)).astype(x.dtype)


def head_rms(x):
    xf = x.astype(jnp.float32)
    return xf * lax.rsqrt(jnp.mean(xf * xf, axis=-1, keepdims=True) + EPS)


def rotary(x, pos):
    d = x.shape[-1]
    inv_freq = 1.0 / (ROPE_BASE ** jnp.linspace(0.0, 1.0, d // 2, dtype=jnp.float32))
    ang = pos.astype(jnp.float32)[:, None] * inv_freq[None, :]
    cos = jnp.cos(ang)[None, :, None, :]
    sin = jnp.sin(ang)[None, :, None, :]
    xf = x.astype(jnp.float32)
    x1, x2 = xf[..., 0::2], xf[..., 1::2]
    out = jnp.stack([x1 * cos - x2 * sin, x2 * cos + x1 * sin], axis=-1)
    return out.reshape(x.shape).astype(x.dtype)


def mix_inputs(h, w_in, q_norm_g, k_norm_g, pos):
    B, T, _ = h.shape
    offsets = np.cumsum(SPLITS)[:-1].tolist()
    rq, rk, rv, rg, aq, ak, av, iq, ik, iw, ga, gb = jnp.split(h @ w_in, offsets, axis=-1)
    rq = rotary(rq.reshape(B, T, RET_HEADS, RET_DK), pos)
    rk = rotary(rk.reshape(B, T, RET_HEADS, RET_DK), pos) * (RET_DK ** -0.5)
    rv = rv.reshape(B, T, RET_HEADS, RET_DV)
    aq = rmsnorm(aq.reshape(B, T, ATT_HEADS, ATT_DH), q_norm_g)
    ak = rmsnorm(ak.reshape(B, T, ATT_HEADS, ATT_DH), k_norm_g)
    av = av.reshape(B, T, ATT_HEADS, ATT_DH)
    iq = iq.reshape(B, T, IDX_HEADS, IDX_DIM)
    return rq, rk, rv, rg, aq, ak, av, iq, ik, iw, ga, gb


def retention_chunk(state, q, k, v, log_gamma):
    C = q.shape[1]
    i = jnp.arange(C, dtype=jnp.float32)
    rel = i[:, None] - i[None, :]
    causal = rel >= 0
    decay = jnp.where(causal[None], jnp.exp(jnp.where(causal, rel, 0.0)[None] * log_gamma[:, None, None]), 0.0)
    qf, kf, vf = q.astype(jnp.float32), k.astype(jnp.float32), v.astype(jnp.float32)
    scores = jnp.einsum('bihd,bjhd->bhij', qf, kf) * decay[None]
    intra = jnp.einsum('bhij,bjhv->bihv', scores, vf)
    cross_decay = jnp.exp((i[:, None] + 1.0) * log_gamma[None, :])
    cross = jnp.einsum('bihd,bhdv->bihv', qf, state) * cross_decay[None, :, :, None]
    k_decay = jnp.exp((C - 1.0 - i)[:, None] * log_gamma[None, :])
    new_state = jnp.exp(C * log_gamma)[None, :, None, None] * state + jnp.einsum('bjhd,bjhv->bhdv', kf * k_decay[None, :, :, None], vf)
    return new_state, intra + cross


def retention_prompt(q, k, v, log_gamma):
    B, S, H, DK = q.shape
    nc = S // RET_CHUNK

    def to_chunks(a):
        return a.reshape((B, nc, RET_CHUNK) + a.shape[2:]).swapaxes(0, 1)

    st0 = jnp.zeros((B, H, DK, v.shape[-1]), jnp.float32)
    st, out = lax.scan(lambda s, c: retention_chunk(s, c[0], c[1], c[2], log_gamma), st0,
                       (to_chunks(q), to_chunks(k), to_chunks(v)))
    return out.swapaxes(0, 1).reshape(B, S, H, v.shape[-1]), st


def indexer_scores(qi, ki, wi):
    s = jnp.einsum('bqhd,bsd->bqhs', qi.astype(jnp.float32), ki.astype(jnp.float32)) * (IDX_DIM ** -0.5)
    return jnp.einsum('bqhs,bqh->bqs', jax.nn.relu(s), wi.astype(jnp.float32) * (IDX_HEADS ** -0.5))


def gather_rows(rows, sel):
    return jax.vmap(lambda r, s: r[s])(rows, sel)


def sparse_attend(q, ks, vs, valid):
    logits = jnp.einsum('bqhd,bqkhd->bqhk', q.astype(jnp.float32), ks.astype(jnp.float32)) * (ATT_DH ** -0.5)
    logits = jnp.where(valid[:, :, None, :], logits, -jnp.inf)
    p = jax.nn.softmax(logits, axis=-1)
    return jnp.einsum('bqhk,bqkhd->bqhd', p, vs.astype(jnp.float32)).astype(q.dtype)


def dsa_prompt(q, k, v, qi, ki, wi):
    B, S = q.shape[:2]
    ksel = min(TOPK_MAX, S // 4)
    nb = S // Q_BLOCK
    keypos = jnp.arange(S, dtype=jnp.int32)

    def to_blocks(a):
        return a.reshape((B, nb, Q_BLOCK) + a.shape[2:]).swapaxes(0, 1)

    def block(args):
        qb, qib, wib, start = args
        pos = start + jnp.arange(Q_BLOCK, dtype=jnp.int32)
        sc = indexer_scores(qib, ki, wib)
        sc = jnp.where(keypos[None, None, :] <= pos[None, :, None], sc, -jnp.inf)
        _, sel = lax.top_k(sc, ksel)
        valid = sel <= pos[None, :, None]
        return sparse_attend(qb, gather_rows(k, sel), gather_rows(v, sel), valid)

    starts = jnp.arange(nb, dtype=jnp.int32) * Q_BLOCK
    out = lax.map(block, (to_blocks(q), to_blocks(qi), to_blocks(wi), starts))
    return out.swapaxes(0, 1).reshape(q.shape)


def dsa_sample(q, k_new, v_new, qi, ki_new, wi, cache_k, cache_v, cache_idx_k, page_table, layer):
    DB, T = q.shape[:2]
    past_len = page_table.shape[1] * PAGE_SIZE
    L = past_len + T
    ksel = min(TOPK_MAX, L // 4)
    ki_past = cache_idx_k[layer, page_table].reshape(DB, past_len, IDX_DIM).astype(ki_new.dtype)
    ki_all = jnp.concatenate([ki_past, ki_new], axis=1)
    pos = past_len + jnp.arange(T, dtype=jnp.int32)
    keypos = jnp.arange(L, dtype=jnp.int32)
    sc = indexer_scores(qi, ki_all, wi)
    sc = jnp.where(keypos[None, None, :] <= pos[None, :, None], sc, -jnp.inf)
    _, sel = lax.top_k(sc, ksel)
    valid = sel <= pos[None, :, None]
    from_past = (sel < past_len)[..., None, None]
    ps = jnp.minimum(sel, past_len - 1)
    phys = jax.vmap(lambda pt, s: pt[s])(page_table, ps // PAGE_SIZE)
    off = ps % PAGE_SIZE
    ns = jnp.clip(sel - past_len, 0, T - 1)
    ks = jnp.where(from_past, cache_k[layer, phys, off].astype(k_new.dtype), gather_rows(k_new, ns))
    vs = jnp.where(from_past, cache_v[layer, phys, off].astype(v_new.dtype), gather_rows(v_new, ns))
    return sparse_attend(q, ks, vs, valid)


def merge_out(x, ret_o, rg, att_o, ga, gb, w_ret_o, w_att_o, w_out):
    B, T, _ = x.shape
    yr = (head_rms(ret_o).reshape(B, T, RET_V) * jax.nn.silu(rg.astype(jnp.float32))).astype(x.dtype)
    ya = att_o.reshape(B, T, ATT_W)
    m = jax.nn.sigmoid(ga) * (yr @ w_ret_o) + jax.nn.sigmoid(gb) * (ya @ w_att_o)
    return x + m @ w_out


def swiglu_ffn(x, g, w_gate, w_up, w_down):
    h = rmsnorm(x, g)
    return x + (jax.nn.silu(h @ w_gate) * (h @ w_up)) @ w_down


def setup_inputs(seed: int = 0) -> dict:
    key = jax.random.key(seed)
    ks = jax.random.split(key, 20)
    n_pages = PAST_LEN // PAGE_SIZE
    n_used = DEC_BATCH * n_pages
    n_pool = n_used + max(1, n_used // 4)

    def nrm(k, shape, scale=1.0):
        return jax.random.normal(k, shape, jnp.float32) * scale

    page_table = jax.random.permutation(ks[0], n_pool)[:n_used].reshape(DEC_BATCH, n_pages).astype(jnp.int32)
    return {
        'x_prompt': nrm(ks[1], (BATCH, SEQ, D_MODEL)),
        'x_sample': nrm(ks[2], (DEC_BATCH, DEC_SEQ, D_MODEL)),
        'cache_k': nrm(ks[3], (DEPTH, n_pool, PAGE_SIZE, ATT_HEADS, ATT_DH)),
        'cache_v': nrm(ks[4], (DEPTH, n_pool, PAGE_SIZE, ATT_HEADS, ATT_DH)),
        'cache_idx_k': nrm(ks[5], (DEPTH, n_pool, PAGE_SIZE, IDX_DIM)),
        'state_ret': nrm(ks[6], (DEPTH, DEC_BATCH, RET_HEADS, RET_DK, RET_DV), RET_DK ** -0.5),
        'page_table': page_table,
        'norm_mix_g': 1.0 + nrm(ks[7], (DEPTH, D_MODEL), 0.01),
        'w_in': nrm(ks[8], (DEPTH, D_MODEL, IN_COLS), D_MODEL ** -0.5),
        'q_norm_g': 1.0 + nrm(ks[9], (DEPTH, ATT_DH), 0.01),
        'k_norm_g': 1.0 + nrm(ks[10], (DEPTH, ATT_DH), 0.01),
        'w_ret_o': nrm(ks[11], (DEPTH, RET_V, D_MODEL), RET_V ** -0.5),
        'w_att_o': nrm(ks[12], (DEPTH, ATT_W, D_MODEL), ATT_W ** -0.5),
        'w_out': nrm(ks[13], (DEPTH, D_MODEL, D_MODEL), D_MODEL ** -0.5),
        'norm_ffn_g': 1.0 + nrm(ks[14], (DEPTH, D_MODEL), 0.01),
        'w_ffn_gate': nrm(ks[15], (DEPTH, D_MODEL, D_FF), D_MODEL ** -0.5),
        'w_ffn_up': nrm(ks[16], (DEPTH, D_MODEL, D_FF), D_MODEL ** -0.5),
        'w_ffn_down': nrm(ks[17], (DEPTH, D_FF, D_MODEL), D_FF ** -0.5),
    }


def reference(x_prompt, x_sample, cache_k, cache_v, cache_idx_k, state_ret, page_table,
              norm_mix_g, w_in, q_norm_g, k_norm_g, w_ret_o, w_att_o, w_out,
              norm_ffn_g, w_ffn_gate, w_ffn_up, w_ffn_down):
    log_gamma = jnp.log1p(-jnp.exp2(-5.0 - jnp.arange(RET_HEADS, dtype=jnp.float32)))
    past_len = page_table.shape[1] * PAGE_SIZE
    pos_p = jnp.arange(x_prompt.shape[1], dtype=jnp.int32)
    pos_s = past_len + jnp.arange(x_sample.shape[1], dtype=jnp.int32)
    xp, xs = x_prompt, x_sample
    kp, vp, ikp, stp = [], [], [], []
    ksm, vsm, iks, sts = [], [], [], []
    for layer in range(DEPTH):
        rq, rk, rv, rg, aq, ak, av, iq, ik, iw, ga, gb = mix_inputs(
            rmsnorm(xp, norm_mix_g[layer]), w_in[layer], q_norm_g[layer], k_norm_g[layer], pos_p)
        ret_o, st_p = retention_prompt(rq, rk, rv, log_gamma)
        att_o = dsa_prompt(aq, ak, av, iq, ik, iw)
        xp = merge_out(xp, ret_o, rg, att_o, ga, gb, w_ret_o[layer], w_att_o[layer], w_out[layer])
        xp = swiglu_ffn(xp, norm_ffn_g[layer], w_ffn_gate[layer], w_ffn_up[layer], w_ffn_down[layer])
        kp.append(ak)
        vp.append(av)
        ikp.append(ik)
        stp.append(st_p.astype(state_ret.dtype))
        rq, rk, rv, rg, aq, ak, av, iq, ik, iw, ga, gb = mix_inputs(
            rmsnorm(xs, norm_mix_g[layer]), w_in[layer], q_norm_g[layer], k_norm_g[layer], pos_s)
        st_s, ret_o = retention_chunk(state_ret[layer].astype(jnp.float32), rq, rk, rv, log_gamma)
        att_o = dsa_sample(aq, ak, av, iq, ik, iw, cache_k, cache_v, cache_idx_k, page_table, layer)
        xs = merge_out(xs, ret_o, rg, att_o, ga, gb, w_ret_o[layer], w_att_o[layer], w_out[layer])
        xs = swiglu_ffn(xs, norm_ffn_g[layer], w_ffn_gate[layer], w_ffn_up[layer], w_ffn_down[layer])
        ksm.append(ak)
        vsm.append(av)
        iks.append(ik)
        sts.append(st_s.astype(state_ret.dtype))
    return (xp, xs, jnp.stack(kp), jnp.stack(vp), jnp.stack(ikp), jnp.stack(stp),
            jnp.stack(ksm), jnp.stack(vsm), jnp.stack(iks), jnp.stack(sts))
```

```python
import functools

import numpy as np
import jax
import jax.numpy as jnp
from jax import lax
from jax.experimental import pallas as pl
from jax.experimental.pallas import tpu as pltpu

F32 = jnp.float32
BF16 = jnp.bfloat16
I32 = jnp.int32

RET_HEADS = 8
RET_DK = 256
RET_DV = 512
RET_CHUNK = 128
ROPE_BASE = 10000.0
ATT_HEADS = 16
ATT_DH = 128
IDX_HEADS = 8
IDX_DIM = 128
TOPK_MAX = 256
Q_BLOCK = 128
EPS = 1e-6

LANES = 128
VMEM_LIMIT_BYTES = 56 * 1024 * 1024

NEG_BIAS = -1e30
INT_MIN = -(2 ** 31)


def _cparams(*sem):
    return pltpu.CompilerParams(dimension_semantics=sem, vmem_limit_bytes=VMEM_LIMIT_BYTES)


def _pick(n, prefs):
    for p in prefs:
        if p <= n and n % p == 0:
            return p
    return n


def _dot(a, b):
    return jnp.dot(a, b, preferred_element_type=F32)


def _dot_nt(a, b):
    return lax.dot_general(a, b, (((1,), (1,)), ((), ())), preferred_element_type=F32)


def _sigmoid(x):
    return 1.0 / (1.0 + jnp.exp(-x))


def _silu(x):
    return x * _sigmoid(x)


def _rmsnorm_kernel(x_ref, g_ref, o_ref):
    x = x_ref[...]
    ms = jnp.mean(x * x, axis=-1, keepdims=True)
    o_ref[...] = (x * lax.rsqrt(ms + EPS) * g_ref[...]).astype(o_ref.dtype)


def _rmsnorm(x, g):
    M, D = x.shape
    tm = _pick(M, (256, 128, 64, 32, 16, 8))
    return pl.pallas_call(
        _rmsnorm_kernel,
        grid=(M // tm,),
        in_specs=[pl.BlockSpec((tm, D), lambda i: (i, 0)), pl.BlockSpec((1, D), lambda i: (0, 0))],
        out_specs=pl.BlockSpec((tm, D), lambda i: (i, 0)),
        out_shape=jax.ShapeDtypeStruct((M, D), BF16),
        compiler_params=_cparams("parallel"),
        name="rmsnorm",
    )(x, g.reshape(1, D))


def _mm_kernel(*refs, n_a, b_src, n_e, epilogue):
    a_refs = refs[:n_a]
    b_refs = refs[n_a:n_a + len(b_src)]
    e_refs = refs[n_a + len(b_src):n_a + len(b_src) + n_e]
    o_refs = refs[n_a + len(b_src) + n_e:]
    accs = [_dot(a_refs[s][...], b[...]) for s, b in zip(b_src, b_refs)]
    res = epilogue(accs, [e[...] for e in e_refs])
    for o, r in zip(o_refs, res):
        o[...] = r.astype(o.dtype)


def _mm(a_list, b_list, extras, epilogue, out_dtypes, *, tm, tn, name):
    M = a_list[0].shape[0]
    N = b_list[0][1].shape[1]
    in_specs, args = [], []
    for a in a_list:
        in_specs.append(pl.BlockSpec((tm, a.shape[1]), lambda i, j: (i, 0)))
        args.append(a)
    for _, b in b_list:
        in_specs.append(pl.BlockSpec((b.shape[0], tn), lambda i, j: (0, j)))
        args.append(b)
    for arr, kind in extras:
        if kind == "tile":
            in_specs.append(pl.BlockSpec((tm, tn), lambda i, j: (i, j)))
        elif kind == "rows":
            in_specs.append(pl.BlockSpec((tm, arr.shape[1]), lambda i, j: (i, 0)))
        else:
            in_specs.append(pl.BlockSpec(arr.shape, lambda i, j, nd=arr.ndim: (0,) * nd))
        args.append(arr)
    kern = functools.partial(_mm_kernel, n_a=len(a_list), b_src=tuple(s for s, _ in b_list),
                             n_e=len(extras), epilogue=epilogue)
    outs = pl.pallas_call(
        kern,
        grid=(M // tm, N // tn),
        in_specs=in_specs,
        out_specs=[pl.BlockSpec((tm, tn), lambda i, j: (i, j)) for _ in out_dtypes],
        out_shape=[jax.ShapeDtypeStruct((M, N), dt) for dt in out_dtypes],
        compiler_params=_cparams("parallel", "arbitrary"),
        name=name,
    )(*args)
    return outs


def _epi_identity(accs, extras):
    return [accs[0]]


def _epi_two_copies(accs, extras):
    return [accs[0], accs[0]]


def _epi_rotary(scale, accs, extras):
    acc = accs[0]
    cos, sin_signed = extras
    dk = cos.shape[1]
    lane = lax.broadcasted_iota(I32, (acc.shape[0], dk), 1)
    even = (lane % 2) == 0
    outs = []
    for h in range(acc.shape[1] // dk):
        x = acc[:, h * dk:(h + 1) * dk]
        swapped = jnp.where(even, pltpu.roll(x, dk - 1, 1), pltpu.roll(x, 1, 1))
        outs.append((x * cos + swapped * sin_signed) * scale)
    return [jnp.concatenate(outs, axis=1) if len(outs) > 1 else outs[0]]


def _epi_head_norm(n_out, accs, extras):
    acc = accs[0]
    g = extras[0]
    dh = g.shape[1]
    outs = []
    for h in range(acc.shape[1] // dh):
        x = acc[:, h * dh:(h + 1) * dh]
        ms = jnp.mean(x * x, axis=-1, keepdims=True)
        outs.append(x * lax.rsqrt(ms + EPS) * g)
    y = jnp.concatenate(outs, axis=1) if len(outs) > 1 else outs[0]
    return [y] * n_out


def _epi_merge(accs, extras):
    ga, gb = extras
    return [_sigmoid(ga) * accs[0] + _sigmoid(gb) * accs[1]]


def _epi_residual(accs, extras):
    return [extras[0] + accs[0]]


def _epi_swiglu(accs, extras):
    return [_silu(accs[0]) * accs[1]]


def _ret_prompt_kernel(q_ref, k_ref, v_ref, rg_ref, dec_ref, cd_ref, kd_ref, gc_ref, yr_ref, st_ref,
                       *, nsub, C):
    @pl.when(pl.program_id(1) == 0)
    def _():
        st_ref[...] = jnp.zeros_like(st_ref)

    dec = dec_ref[0]
    cd = cd_ref[0]
    kd = kd_ref[0]
    gc = gc_ref[0]
    for t in range(nsub):
        rows = slice(t * C, (t + 1) * C)
        q = q_ref[rows, :]
        k = k_ref[rows, :]
        v = v_ref[rows, :]
        st = st_ref[0]
        s = _dot_nt(q, k) * dec
        o = _dot(s.astype(BF16), v) + _dot(q, st.astype(BF16)) * cd
        kdt = (k.astype(F32) * kd).T.astype(BF16)
        st_ref[0] = gc * st + _dot(kdt, v)
        ms = jnp.mean(o * o, axis=-1, keepdims=True)
        yr_ref[rows, :] = (o * lax.rsqrt(ms + EPS) * _silu(rg_ref[rows, :])).astype(yr_ref.dtype)


def _retention_prompt(rq, rk, rv, rg, tabs):
    S = rq.shape[0]
    H, DK, DV, C = RET_HEADS, RET_DK, RET_DV, RET_CHUNK
    nsub = _pick(S // C, (4, 2, 1))
    T = nsub * C
    dec, cd, kd, gc = tabs
    kern = functools.partial(_ret_prompt_kernel, nsub=nsub, C=C)
    return pl.pallas_call(
        kern,
        grid=(H, S // T),
        in_specs=[
            pl.BlockSpec((T, DK), lambda h, c: (c, h)),
            pl.BlockSpec((T, DK), lambda h, c: (c, h)),
            pl.BlockSpec((T, DV), lambda h, c: (c, h)),
            pl.BlockSpec((T, DV), lambda h, c: (c, h)),
            pl.BlockSpec((1, C, C), lambda h, c: (h, 0, 0)),
            pl.BlockSpec((1, C, DV), lambda h, c: (h, 0, 0)),
            pl.BlockSpec((1, C, DK), lambda h, c: (h, 0, 0)),
            pl.BlockSpec((1, 1, DV), lambda h, c: (h, 0, 0)),
        ],
        out_specs=[
            pl.BlockSpec((T, DV), lambda h, c: (c, h)),
            pl.BlockSpec((1, DK, DV), lambda h, c: (h, 0, 0)),
        ],
        out_shape=[jax.ShapeDtypeStruct((S, H * DV), BF16), jax.ShapeDtypeStruct((H, DK, DV), F32)],
        compiler_params=_cparams("parallel", "arbitrary"),
        name="retention_prompt",
    )(rq, rk, rv, rg, dec, cd, kd, gc)


def _retention_tables(log_gamma):
    H, DK, DV, C = RET_HEADS, RET_DK, RET_DV, RET_CHUNK
    i = jnp.arange(C, dtype=F32)
    rel = i[:, None] - i[None, :]
    causal = rel >= 0
    dec = jnp.where(causal[None], jnp.exp(jnp.where(causal, rel, 0.0)[None] * log_gamma[:, None, None]), 0.0)
    cross = jnp.exp((i[:, None] + 1.0) * log_gamma[None, :])
    kdec = jnp.exp((C - 1.0 - i)[:, None] * log_gamma[None, :])
    gC = jnp.exp(C * log_gamma)
    cd = jnp.broadcast_to(cross.T[:, :, None], (H, C, DV))
    kd = jnp.broadcast_to(kdec.T[:, :, None], (H, C, DK))
    gc = jnp.broadcast_to(gC[:, None, None], (H, 1, DV))
    return dec, cd, kd, gc


def _ret_sample_kernel(st_ref, qt_ref, kt_ref, v_ref, rg_ref, g1_ref, o_ref, ns_ref, *, H):
    qt = qt_ref[0]
    kt = kt_ref[0]
    v = v_ref[0]
    rg = rg_ref[0]
    for h in range(H):
        st = st_ref[0, 0, h]
        qc = qt[:, h:h + 1]
        kc = kt[:, h:h + 1]
        vr = v[h:h + 1, :]
        g1 = g1_ref[h]
        cross = jnp.sum(st * qc, axis=0, keepdims=True)
        qk = jnp.sum(qc * kc, axis=0, keepdims=True)
        o = qk * vr + cross * g1
        ns_ref[0, 0, h] = g1 * st + kc * vr
        ms = jnp.mean(o * o, axis=-1, keepdims=True)
        o_ref[0, h:h + 1, :] = o * lax.rsqrt(ms + EPS) * _silu(rg[h:h + 1, :])


def _retention_sample(state_ret, layer, rq, rk, rv, rg, log_gamma):
    DB = rq.shape[0]
    H, DK, DV = RET_HEADS, RET_DK, RET_DV
    qt = rq.reshape(DB, H, DK).transpose(0, 2, 1)
    kt = rk.reshape(DB, H, DK).transpose(0, 2, 1)
    g1 = jnp.broadcast_to(jnp.exp(1.0 * log_gamma)[:, None, None], (H, 1, DV))
    kern = functools.partial(_ret_sample_kernel, H=H)
    yr, ns = pl.pallas_call(
        kern,
        grid=(DB,),
        in_specs=[
            pl.BlockSpec((1, 1, H, DK, DV), lambda b: (layer, b, 0, 0, 0)),
            pl.BlockSpec((1, DK, H), lambda b: (b, 0, 0)),
            pl.BlockSpec((1, DK, H), lambda b: (b, 0, 0)),
            pl.BlockSpec((1, H, DV), lambda b: (b, 0, 0)),
            pl.BlockSpec((1, H, DV), lambda b: (b, 0, 0)),
            pl.BlockSpec((H, 1, DV), lambda b: (0, 0, 0)),
        ],
        out_specs=[
            pl.BlockSpec((1, H, DV), lambda b: (b, 0, 0)),
            pl.BlockSpec((1, 1, H, DK, DV), lambda b: (0, b, 0, 0, 0)),
        ],
        out_shape=[jax.ShapeDtypeStruct((DB, H, DV), F32),
                   jax.ShapeDtypeStruct((1, DB, H, DK, DV), state_ret.dtype)],
        compiler_params=_cparams("parallel"),
        name="retention_sample",
    )(state_ret, qt, kt, rv.reshape(DB, H, DV), rg.reshape(DB, H, DV), g1)
    return yr.reshape(DB, H * DV), ns[0]


def _sortable_key(x):
    bits = pltpu.bitcast(x, I32)
    return bits ^ ((bits >> 31) & 0x7FFFFFFF)


def _topk_select(keys_ref, nvis, ksel, col_bits):
    _, R, CW = keys_ref.shape
    ksel_f = float(ksel)

    def count(pred):
        def body(c, acc):
            f = jnp.where(pred(keys_ref[c], c), 1.0, 0.0)
            for g in range(CW // LANES):
                acc = acc + f[:, g * LANES:(g + 1) * LANES]
            return acc
        acc = lax.fori_loop(0, nvis, body, jnp.zeros((R, LANES), F32))
        return jnp.sum(acc, axis=-1, keepdims=True)

    def value_bit(bi, u):
        cand_u = u | lax.shift_left(jnp.int32(1), 31 - bi)
        cand = cand_u ^ INT_MIN
        cnt = count(lambda key, c: key >= cand)
        return jnp.where(cnt >= ksel_f, cand_u, u)

    u = lax.fori_loop(0, 32, value_bit, jnp.zeros((R, 1), I32))
    T = u ^ INT_MIN
    need = ksel_f - count(lambda key, c: key > T)
    ties = count(lambda key, c: key == T)

    def col_of(c):
        return c * CW + lax.broadcasted_iota(I32, (R, CW), 1)

    def col_search():
        def col_bit(bi, x):
            cand = x | lax.shift_left(jnp.int32(1), col_bits - 1 - bi)
            cnt = count(lambda key, c: (key == T) & (col_of(c) < cand))
            return jnp.where(cnt < need, cand, x)
        return lax.fori_loop(0, col_bits, col_bit, jnp.zeros((R, 1), I32))

    contested = jnp.max(jnp.where((ties > need) & (T != INT_MIN), 1.0, 0.0)) > 0.0
    J = lax.cond(contested, col_search, lambda: jnp.full((R, 1), 2 ** 30, I32))
    J = jnp.where(T == INT_MIN, -1, J)
    return T, J


def _dsa_select_kernel(iq_ref, iw_ref, ik_ref, bias_ref, keys_ref, *, ksel, col_bits):
    NC, QB, CW = keys_ref.shape
    i = pl.program_id(0)
    nvis = ((i + 1) * QB + CW - 1) // CW
    w = iw_ref[...] * (IDX_HEADS ** -0.5 * IDX_DIM ** -0.5)
    rowpos = i * QB + lax.broadcasted_iota(I32, (QB, 1), 0)

    def col_of(c):
        return c * CW + lax.broadcasted_iota(I32, (QB, CW), 1)

    def score_chunk(c, carry):
        kc = ik_ref[c]
        acc = jnp.zeros((QB, CW), F32)
        for h in range(IDX_HEADS):
            s = _dot_nt(iq_ref[:, h * IDX_DIM:(h + 1) * IDX_DIM], kc)
            acc = acc + jnp.maximum(s, 0.0) * w[:, h:h + 1]
        keys_ref[c] = jnp.where(col_of(c) <= rowpos, _sortable_key(acc), INT_MIN)
        return carry

    lax.fori_loop(0, nvis, score_chunk, 0)
    T, J = _topk_select(keys_ref, nvis, ksel, col_bits)

    def emit(c, carry):
        key = keys_ref[c]
        col = col_of(c)
        sel = ((key > T) | ((key == T) & (col <= J))) & (col <= rowpos)
        bias_ref[0, c] = jnp.where(sel, 0.0, NEG_BIAS).astype(bias_ref.dtype)
        return carry

    lax.fori_loop(0, nvis, emit, 0)

    def fill(c, carry):
        bias_ref[0, c] = jnp.full((QB, CW), NEG_BIAS, bias_ref.dtype)
        return carry

    lax.fori_loop(nvis, NC, fill, 0)


def _dsa_select_prompt(iq, iw, ik, ksel, CW):
    S = iq.shape[0]
    QB = Q_BLOCK
    NQ, NC = S // QB, S // CW
    kern = functools.partial(_dsa_select_kernel, ksel=ksel, col_bits=int(np.ceil(np.log2(S))))
    return pl.pallas_call(
        kern,
        grid=(NQ,),
        in_specs=[
            pl.BlockSpec((QB, iq.shape[1]), lambda i: (i, 0)),
            pl.BlockSpec((QB, iw.shape[1]), lambda i: (i, 0)),
            pl.BlockSpec((NC, CW, ik.shape[1]), lambda i: (0, 0, 0)),
        ],
        out_specs=pl.BlockSpec((1, NC, QB, CW), lambda i: (i, 0, 0, 0)),
        out_shape=jax.ShapeDtypeStruct((NQ, NC, QB, CW), BF16),
        scratch_shapes=[pltpu.VMEM((NC, QB, CW), I32)],
        compiler_params=_cparams("parallel"),
        name="dsa_select_prompt",
    )(iq, iw, ik.reshape(NC, CW, ik.shape[1]))


def _dsa_attn_kernel(q_ref, k_ref, v_ref, b_ref, o_ref, m_sc, l_sc, acc_sc, *, H, DH, QB, TK):
    i = pl.program_id(0)
    j = pl.program_id(1)
    last = ((i + 1) * QB - 1) // TK
    scale = DH ** -0.5

    @pl.when(j == 0)
    def _():
        m_sc[...] = jnp.full(m_sc.shape, NEG_BIAS, F32)
        l_sc[...] = jnp.zeros_like(l_sc)
        acc_sc[...] = jnp.zeros_like(acc_sc)

    @pl.when(j <= last)
    def _():
        b = b_ref[0, 0].astype(F32)
        for h in range(H):
            cols = slice(h * DH, (h + 1) * DH)
            s = _dot_nt(q_ref[:, cols], k_ref[:, cols]) * scale + b
            m_prev = m_sc[h]
            m_new = jnp.maximum(m_prev, jnp.max(s, axis=-1, keepdims=True))
            alpha = jnp.exp(m_prev - m_new)
            p = jnp.exp(s - jnp.concatenate([m_new] * (TK // LANES), axis=1))
            l_sc[h] = alpha * l_sc[h] + jnp.sum(p, axis=-1, keepdims=True)
            acc_sc[:, cols] = alpha * acc_sc[:, cols] + _dot(p.astype(BF16), v_ref[:, cols])
            m_sc[h] = m_new

    @pl.when(j == last)
    def _():
        for h in range(H):
            cols = slice(h * DH, (h + 1) * DH)
            o_ref[:, cols] = (acc_sc[:, cols] / l_sc[h]).astype(o_ref.dtype)


def _dsa_attn_prompt(aq, ak, av, bias, TK):
    S, W = aq.shape
    H, DH, QB = ATT_HEADS, ATT_DH, Q_BLOCK
    assert DH == LANES

    def kv_map(i, j):
        return (jnp.minimum(j, ((i + 1) * QB - 1) // TK), 0)

    def b_map(i, j):
        return (i, jnp.minimum(j, ((i + 1) * QB - 1) // TK), 0, 0)

    kern = functools.partial(_dsa_attn_kernel, H=H, DH=DH, QB=QB, TK=TK)
    return pl.pallas_call(
        kern,
        grid=(S // QB, S // TK),
        in_specs=[
            pl.BlockSpec((QB, W), lambda i, j: (i, 0)),
            pl.BlockSpec((TK, W), kv_map),
            pl.BlockSpec((TK, W), kv_map),
            pl.BlockSpec((1, 1, QB, TK), b_map),
        ],
        out_specs=pl.BlockSpec((QB, W), lambda i, j: (i, 0)),
        out_shape=jax.ShapeDtypeStruct((S, W), BF16),
        scratch_shapes=[pltpu.VMEM((H, QB, LANES), F32), pltpu.VMEM((H, QB, LANES), F32),
                        pltpu.VMEM((QB, W), F32)],
        compiler_params=_cparams("parallel", "arbitrary"),
        name="dsa_attn_prompt",
    )(aq, ak, av, bias)


def _idx_sample_kernel(pt_ref, iq_ref, w_ref, page_ref, o_ref):
    page = page_ref[0, 0].astype(BF16)
    s = _dot_nt(iq_ref[0].astype(BF16), page)
    o_ref[0, 0] = jnp.sum(jnp.maximum(s, 0.0) * w_ref[0], axis=0, keepdims=True)


def _idx_sample(cache_idx_k, layer, page_table, iq, iw):
    DB, NP = page_table.shape
    P, Di = cache_idx_k.shape[2:]
    Hi = IDX_HEADS
    w = jnp.broadcast_to((iw * (IDX_HEADS ** -0.5 * IDX_DIM ** -0.5))[:, :, None], (DB, Hi, P))
    return pl.pallas_call(
        _idx_sample_kernel,
        grid_spec=pltpu.PrefetchScalarGridSpec(
            num_scalar_prefetch=1,
            grid=(DB, NP),
            in_specs=[
                pl.BlockSpec((1, Hi, Di), lambda b, p, pt: (b, 0, 0)),
                pl.BlockSpec((1, Hi, P), lambda b, p, pt: (b, 0, 0)),
                pl.BlockSpec((1, 1, P, Di), lambda b, p, pt: (layer, pt[b, p], 0, 0)),
            ],
            out_specs=pl.BlockSpec((1, 1, 1, P), lambda b, p, pt: (b, p, 0, 0)),
        ),
        out_shape=jax.ShapeDtypeStruct((DB, NP, 1, P), F32),
        compiler_params=_cparams("parallel", "arbitrary"),
        name="dsa_index_sample",
    )(page_table, iq.reshape(DB, Hi, Di), w, cache_idx_k)


def _select_sample_kernel(sc_ref, iq_ref, ikn_ref, iw_ref, bias_ref, keys_ref, *, ksel, past_len, col_bits):
    NC, R, CW = keys_ref.shape
    NP = NC - 1
    w = iw_ref[...] * (IDX_HEADS ** -0.5 * IDX_DIM ** -0.5)
    ikn = ikn_ref[...]
    new = jnp.zeros((R, 1), F32)
    for h in range(IDX_HEADS):
        s = jnp.sum(iq_ref[:, h * IDX_DIM:(h + 1) * IDX_DIM] * ikn, axis=-1, keepdims=True)
        new = new + jnp.maximum(s, 0.0) * w[:, h:h + 1]

    def col_of(c):
        return c * CW + lax.broadcasted_iota(I32, (R, CW), 1)

    def load(c, carry):
        keys_ref[c] = _sortable_key(sc_ref[c] + 0.0)
        return carry

    lax.fori_loop(0, NP, load, 0)
    keys_ref[NP] = jnp.where(col_of(NP) <= past_len, _sortable_key(jnp.broadcast_to(new, (R, CW))), INT_MIN)
    T, J = _topk_select(keys_ref, NC, ksel, col_bits)

    def emit(c, carry):
        key = keys_ref[c]
        col = col_of(c)
        sel = ((key > T) | ((key == T) & (col <= J))) & (col <= past_len)
        bias_ref[c] = jnp.where(sel, 0.0, NEG_BIAS)
        return carry

    lax.fori_loop(0, NC, emit, 0)


def _select_sample(scores, iq, ik_new, iw, ksel):
    DB, NP, _, P = scores.shape
    past_len = NP * P
    sc = scores.reshape(DB, NP, P).transpose(1, 0, 2)
    kern = functools.partial(_select_sample_kernel, ksel=ksel, past_len=past_len,
                             col_bits=int(np.ceil(np.log2(past_len + P))))
    return pl.pallas_call(
        kern,
        out_shape=jax.ShapeDtypeStruct((NP + 1, DB, P), F32),
        scratch_shapes=[pltpu.VMEM((NP + 1, DB, P), I32)],
        compiler_params=pltpu.CompilerParams(vmem_limit_bytes=VMEM_LIMIT_BYTES),
        name="dsa_select_sample",
    )(sc, iq, ik_new, iw)


def _attn_sample_kernel(pt_ref, q_ref, kn_ref, vn_ref, b_ref, bn_ref, kp_ref, vp_ref, o_ref,
                        m_sc, l_sc, acc_sc, *, H, DH, P):
    p = pl.program_id(1)
    scale = DH ** -0.5

    @pl.when(p == 0)
    def _():
        m_sc[...] = jnp.full(m_sc.shape, NEG_BIAS, F32)
        l_sc[...] = jnp.zeros_like(l_sc)
        acc_sc[...] = jnp.zeros_like(acc_sc)

    q = q_ref[0]

    def row_to_lead(row):
        eye = (lax.broadcasted_iota(I32, (P, 1, P), 0) == lax.broadcasted_iota(I32, (P, 1, P), 2))
        return jnp.sum(jnp.where(eye, row, 0.0), axis=-1, keepdims=True)

    def update(kk, vv, bias3):
        s = jnp.sum(kk * q[None], axis=-1, keepdims=True) * scale + bias3
        m_prev = m_sc[...]
        m_new = jnp.maximum(m_prev, jnp.max(s, axis=0))
        alpha = jnp.exp(m_prev - m_new)
        pr = jnp.exp(s - m_new[None])
        l_sc[...] = alpha * l_sc[...] + jnp.sum(pr, axis=0)
        acc_sc[...] = alpha * acc_sc[...] + jnp.sum(pr * vv, axis=0)
        m_sc[...] = m_new

    update(kp_ref[0, 0], vp_ref[0, 0], row_to_lead(b_ref[0]))

    @pl.when(p == pl.num_programs(1) - 1)
    def _():
        lane0 = lax.broadcasted_iota(I32, (1, 1, P), 2) == 0
        bnew = jnp.sum(jnp.where(lane0, bn_ref[0], 0.0), axis=-1, keepdims=True)
        update(kn_ref[...], vn_ref[...], bnew)
        o_ref[0] = acc_sc[...] / l_sc[...]


def _attn_sample(cache_k, cache_v, layer, page_table, aq, ak_new, av_new, bias):
    DB, NP = page_table.shape
    P, H, DH = cache_k.shape[2:]
    b_pages = bias[:NP].transpose(1, 0, 2).reshape(DB * NP, 1, P)
    b_new = bias[NP].reshape(DB, 1, P)
    kern = functools.partial(_attn_sample_kernel, H=H, DH=DH, P=P)
    out = pl.pallas_call(
        kern,
        grid_spec=pltpu.PrefetchScalarGridSpec(
            num_scalar_prefetch=1,
            grid=(DB, NP),
            in_specs=[
                pl.BlockSpec((1, H, DH), lambda b, p, pt: (b, 0, 0)),
                pl.BlockSpec((1, H, DH), lambda b, p, pt: (b, 0, 0)),
                pl.BlockSpec((1, H, DH), lambda b, p, pt: (b, 0, 0)),
                pl.BlockSpec((1, 1, P), lambda b, p, pt: (b * NP + p, 0, 0)),
                pl.BlockSpec((1, 1, P), lambda b, p, pt: (b, 0, 0)),
                pl.BlockSpec((1, 1, P, H, DH), lambda b, p, pt: (layer, pt[b, p], 0, 0, 0)),
                pl.BlockSpec((1, 1, P, H, DH), lambda b, p, pt: (layer, pt[b, p], 0, 0, 0)),
            ],
            out_specs=pl.BlockSpec((1, H, DH), lambda b, p, pt: (b, 0, 0)),
            scratch_shapes=[pltpu.VMEM((H, 1), F32), pltpu.VMEM((H, 1), F32), pltpu.VMEM((H, DH), F32)],
        ),
        out_shape=jax.ShapeDtypeStruct((DB, H, DH), F32),
        compiler_params=_cparams("parallel", "arbitrary"),
        name="dsa_attn_sample",
    )(page_table, aq.reshape(DB, H, DH), ak_new.reshape(DB, H, DH), av_new.reshape(DB, H, DH),
      b_pages, b_new, cache_k, cache_v)
    return out.reshape(DB, H * DH)


def _rotary_tables(pos):
    d = RET_DK
    inv_freq = 1.0 / (ROPE_BASE ** jnp.linspace(0.0, 1.0, d // 2, dtype=F32))
    ang = pos.astype(F32)[:, None] * inv_freq[None, :]
    cos = jnp.repeat(jnp.cos(ang), 2, axis=1)
    sin = jnp.sin(ang)
    sin_signed = jnp.stack([-sin, sin], axis=-1).reshape(pos.shape[0], d)
    return cos, sin_signed


def _split_weights(w):
    rqk, rv_w = RET_HEADS * RET_DK, RET_HEADS * RET_DV
    aw, iqw = ATT_HEADS * ATT_DH, IDX_HEADS * IDX_DIM
    d_model = w.shape[0]
    sizes = (rqk, rqk, rv_w, rv_w, aw, aw, aw, iqw, IDX_DIM, IDX_HEADS, d_model, d_model)
    offs = np.concatenate([[0], np.cumsum(sizes)]).tolist()
    assert offs[-1] == w.shape[1]
    names = ("rq", "rk", "rv", "rg", "aq", "ak", "av", "iq", "ik", "iw", "ga", "gb")
    out = {n: w[:, offs[t]:offs[t + 1]].astype(BF16) for t, n in enumerate(names) if n not in ("ik", "iw")}
    ikw = w[:, offs[8]:offs[10]]
    out["ikw"] = jnp.pad(ikw, ((0, 0), (0, 2 * LANES - ikw.shape[1]))).astype(BF16)
    return out


def _mix_inputs(h, ws, qg, kg, cos, sin_signed, act_dt):
    M = h.shape[0]
    tm = _pick(M, (512, 256, 128, 64, 32, 16, 8))

    def proj(name, epi, extras, dts, tn_prefs=(512, 256, 128)):
        w = ws[name]
        return _mm([h], [(0, w)], extras, epi, dts, tm=tm, tn=_pick(w.shape[1], tn_prefs), name="proj_" + name)

    rot = [(cos, "rows"), (sin_signed, "rows")]
    (rq,) = proj("rq", functools.partial(_epi_rotary, 1.0), rot, [act_dt])
    (rk,) = proj("rk", functools.partial(_epi_rotary, RET_DK ** -0.5), rot, [act_dt])
    (rv,) = proj("rv", _epi_identity, [], [act_dt])
    (rg,) = proj("rg", _epi_identity, [], [F32])
    (aq,) = proj("aq", functools.partial(_epi_head_norm, 1), [(qg.reshape(1, -1), "full")], [act_dt])
    ak, ak16 = proj("ak", functools.partial(_epi_head_norm, 2), [(kg.reshape(1, -1), "full")], [F32, BF16])
    av, av16 = proj("av", _epi_two_copies, [], [F32, BF16])
    (iq,) = proj("iq", _epi_identity, [], [act_dt])
    (ikw,) = proj("ikw", _epi_identity, [], [F32], tn_prefs=(256,))
    (ga,) = proj("ga", _epi_identity, [], [F32])
    (gb,) = proj("gb", _epi_identity, [], [F32])
    ik = ikw[:, :IDX_DIM]
    iw = ikw[:, IDX_DIM:IDX_DIM + IDX_HEADS]
    return rq, rk, rv, rg, aq, ak, ak16, av, av16, iq, ik, iw, ga, gb


def _merge_ffn(x, yr, ya, ga, gb, wr, wa, wo, g_ffn, wg, wu, wd):
    M = x.shape[0]
    tm = _pick(M, (512, 256, 128, 64, 32, 16, 8))
    D = x.shape[1]
    tn = _pick(D, (512, 256, 128))
    (m,) = _mm([yr, ya], [(0, wr), (1, wa)], [(ga, "tile"), (gb, "tile")], _epi_merge, [BF16],
               tm=tm, tn=tn, name="merge")
    (x1,) = _mm([m], [(0, wo)], [(x, "tile")], _epi_residual, [F32], tm=tm, tn=tn, name="out_proj")
    hf = _rmsnorm(x1, g_ffn)
    tf = _pick(wg.shape[1], (256, 128))
    (u,) = _mm([hf], [(0, wg), (0, wu)], [], _epi_swiglu, [BF16], tm=tm, tn=tf, name="ffn_up")
    (y,) = _mm([u], [(0, wd)], [(x1, "tile")], _epi_residual, [F32], tm=tm, tn=_pick(D, (256, 128)),
               name="ffn_down")
    return y


def kernel(x_prompt, x_sample, cache_k, cache_v, cache_idx_k, state_ret, page_table, norm_mix_g, w_in,
           q_norm_g, k_norm_g, w_ret_o, w_att_o, w_out, norm_ffn_g, w_ffn_gate, w_ffn_up, w_ffn_down):
    depth = w_in.shape[0]
    B, S, D = x_prompt.shape
    DB, T, _ = x_sample.shape
    assert B == 1 and T == 1
    page_size = cache_k.shape[2]
    past_len = page_table.shape[1] * page_size

    log_gamma = jnp.log1p(-jnp.exp2(-5.0 - jnp.arange(RET_HEADS, dtype=F32)))
    ret_tabs = _retention_tables(log_gamma)
    cos_p, sin_p = _rotary_tables(jnp.arange(S, dtype=I32))
    cos_s, sin_s = _rotary_tables(jnp.full((DB,), past_len, I32))
    ksel_p = min(TOPK_MAX, S // 4)
    ksel_s = min(TOPK_MAX, (past_len + T) // 4)
    cw = _pick(S, (512, 256, 128))

    xp = x_prompt.reshape(S, D)
    xs = x_sample.reshape(DB, D)
    kp, vp, ikp, stp, ksm, vsm, iks, sts = [], [], [], [], [], [], [], []
    for layer in range(depth):
        ws = _split_weights(w_in[layer])
        wr, wa, wo = (w[layer].astype(BF16) for w in (w_ret_o, w_att_o, w_out))
        wg, wu, wd = (w[layer].astype(BF16) for w in (w_ffn_gate, w_ffn_up, w_ffn_down))

        h = _rmsnorm(xp, norm_mix_g[layer])
        rq, rk, rv, rg, aq, ak, ak16, av, av16, iq, ik, iw, ga, gb = _mix_inputs(
            h, ws, q_norm_g[layer], k_norm_g[layer], cos_p, sin_p, BF16)
        yr, st_p = _retention_prompt(rq, rk, rv, rg, ret_tabs)
        bias = _dsa_select_prompt(iq, iw, ik.astype(BF16), ksel_p, cw)
        ya = _dsa_attn_prompt(aq, ak16, av16, bias, cw)
        xp = _merge_ffn(xp, yr, ya, ga, gb, wr, wa, wo, norm_ffn_g[layer], wg, wu, wd)
        kp.append(ak.reshape(B, S, ATT_HEADS, ATT_DH))
        vp.append(av.reshape(B, S, ATT_HEADS, ATT_DH))
        ikp.append(ik.reshape(B, S, IDX_DIM))
        stp.append(st_p[None].astype(state_ret.dtype))

        h = _rmsnorm(xs, norm_mix_g[layer])
        rq, rk, rv, rg, aq, ak, ak16, av, av16, iq, ik, iw, ga, gb = _mix_inputs(
            h, ws, q_norm_g[layer], k_norm_g[layer], cos_s, sin_s, F32)
        yr, st_s = _retention_sample(state_ret, layer, rq, rk, rv, rg, log_gamma)
        scores = _idx_sample(cache_idx_k, layer, page_table, iq, iw)
        bias = _select_sample(scores, iq, ik, iw, ksel_s)
        ya = _attn_sample(cache_k, cache_v, layer, page_table, aq, ak, av, bias)
        xs = _merge_ffn(xs, yr.astype(BF16), ya.astype(BF16), ga, gb, wr, wa, wo, norm_ffn_g[layer], wg, wu, wd)
        ksm.append(ak.reshape(DB, T, ATT_HEADS, ATT_DH))
        vsm.append(av.reshape(DB, T, ATT_HEADS, ATT_DH))
        iks.append(ik.reshape(DB, T, IDX_DIM))
        sts.append(st_s)
    return (xp.reshape(B, S, D), xs.reshape(DB, T, D), jnp.stack(kp), jnp.stack(vp), jnp.stack(ikp),
            jnp.stack(stp), jnp.stack(ksm), jnp.stack(vsm), jnp.stack(iks), jnp.stack(sts))
```

```python
import functools

import numpy as np
import jax
import jax.numpy as jnp
from jax import lax
from jax.experimental import pallas as pl
from jax.experimental.pallas import tpu as pltpu

F32 = jnp.float32
BF16 = jnp.bfloat16
I32 = jnp.int32

RET_HEADS = 8
RET_DK = 256
RET_DV = 512
RET_CHUNK = 128
ROPE_BASE = 10000.0
ATT_HEADS = 16
ATT_DH = 128
IDX_HEADS = 8
IDX_DIM = 128
TOPK_MAX = 256
Q_BLOCK = 128
EPS = 1e-6

LANES = 128
VMEM_LIMIT_BYTES = 56 * 1024 * 1024

NEG_BIAS = -1e30
INT_MIN = -(2 ** 31)


def _cparams(*sem):
    return pltpu.CompilerParams(dimension_semantics=sem, vmem_limit_bytes=VMEM_LIMIT_BYTES)


def _pick(n, prefs):
    for p in prefs:
        if p <= n and n % p == 0:
            return p
    return n


def _dot(a, b):
    return jnp.dot(a, b, preferred_element_type=F32)


def _dot_nt(a, b):
    return lax.dot_general(a, b, (((1,), (1,)), ((), ())), preferred_element_type=F32)


def _sigmoid(x):
    return 1.0 / (1.0 + jnp.exp(-x))


def _silu(x):
    return x * _sigmoid(x)


def _rmsnorm_kernel(x_ref, g_ref, o_ref):
    x = x_ref[...]
    ms = jnp.mean(x * x, axis=-1, keepdims=True)
    o_ref[...] = (x * lax.rsqrt(ms + EPS) * g_ref[...]).astype(o_ref.dtype)


def _rmsnorm(x, g):
    M, D = x.shape
    tm = _pick(M, (256, 128, 64, 32, 16, 8))
    return pl.pallas_call(
        _rmsnorm_kernel,
        grid=(M // tm,),
        in_specs=[pl.BlockSpec((tm, D), lambda i: (i, 0)), pl.BlockSpec((1, D), lambda i: (0, 0))],
        out_specs=pl.BlockSpec((tm, D), lambda i: (i, 0)),
        out_shape=jax.ShapeDtypeStruct((M, D), BF16),
        compiler_params=_cparams("parallel"),
        name="rmsnorm",
    )(x, g.reshape(1, D))


def _mm_kernel(*refs, n_a, b_src, n_e, n_o, cast, epilogue):
    n_b = len(b_src)
    a_refs = refs[:n_a]
    b_refs = refs[n_a:n_a + n_b]
    e_refs = refs[n_a + n_b:n_a + n_b + n_e]
    o_refs = refs[n_a + n_b + n_e:n_a + n_b + n_e + n_o]
    if cast:
        w_refs = refs[n_a + n_b + n_e + n_o:]

        @pl.when(pl.program_id(1) == 0)
        def _():
            for b, w in zip(b_refs, w_refs):
                w[...] = b[...].astype(w.dtype)
        b_refs = w_refs
    accs = [_dot(a_refs[s][...], b[...]) for s, b in zip(b_src, b_refs)]
    res = epilogue(accs, [e[...] for e in e_refs])
    for o, r in zip(o_refs, res):
        o[...] = r.astype(o.dtype)


def _mm(a_list, b_list, extras, epilogue, out_dtypes, *, tm, tn, name, hold_b=True):
    M = a_list[0].shape[0]
    N = b_list[0][4]
    cast = b_list[0][2] is not None
    assert all((b[2] is not None) == cast for b in b_list) and (hold_b or not cast)
    if hold_b:
        grid = (N // tn, M // tm)
        ij = lambda f: (lambda j, i: f(i, j))
    else:
        grid = (M // tm, N // tn)
        ij = lambda f: f
    in_specs, args, scratch = [], [], []
    for a in a_list:
        in_specs.append(pl.BlockSpec((tm, a.shape[1]), ij(lambda i, j: (i, 0))))
        args.append(a)
    for _, w, layer, col0, _ in b_list:
        assert col0 % tn == 0
        if cast:
            K = w.shape[1]
            in_specs.append(pl.BlockSpec((None, K, tn), ij(lambda i, j, l=layer, c=col0 // tn: (l, 0, c + j))))
            scratch.append(pltpu.VMEM((K, tn), BF16))
        else:
            in_specs.append(pl.BlockSpec((w.shape[0], tn), ij(lambda i, j, c=col0 // tn: (0, c + j))))
        args.append(w)
    for arr, kind in extras:
        if kind == "tile":
            in_specs.append(pl.BlockSpec((tm, tn), ij(lambda i, j: (i, j))))
        elif kind == "rows":
            in_specs.append(pl.BlockSpec((tm, arr.shape[1]), ij(lambda i, j: (i, 0))))
        else:
            in_specs.append(pl.BlockSpec(arr.shape, ij(lambda i, j, nd=arr.ndim: (0,) * nd)))
        args.append(arr)
    kern = functools.partial(_mm_kernel, n_a=len(a_list), b_src=tuple(b[0] for b in b_list),
                             n_e=len(extras), n_o=len(out_dtypes), cast=cast, epilogue=epilogue)
    outs = pl.pallas_call(
        kern,
        grid=grid,
        in_specs=in_specs,
        out_specs=[pl.BlockSpec((tm, tn), ij(lambda i, j: (i, j))) for _ in out_dtypes],
        out_shape=[jax.ShapeDtypeStruct((M, N), dt) for dt in out_dtypes],
        scratch_shapes=scratch,
        compiler_params=_cparams("parallel", "arbitrary"),
        name=name,
    )(*args)
    return outs


def _epi_identity(accs, extras):
    return [accs[0]]


def _epi_two_copies(accs, extras):
    return [accs[0], accs[0]]


def _epi_rotary(scale, accs, extras):
    acc = accs[0]
    cos, sin_signed = extras
    dk = cos.shape[1]
    lane = lax.broadcasted_iota(I32, (acc.shape[0], dk), 1)
    even = (lane % 2) == 0
    outs = []
    for h in range(acc.shape[1] // dk):
        x = acc[:, h * dk:(h + 1) * dk]
        swapped = jnp.where(even, pltpu.roll(x, dk - 1, 1), pltpu.roll(x, 1, 1))
        outs.append((x * cos + swapped * sin_signed) * scale)
    return [jnp.concatenate(outs, axis=1) if len(outs) > 1 else outs[0]]


def _epi_head_norm(n_out, accs, extras):
    acc = accs[0]
    g = extras[0]
    dh = g.shape[1]
    outs = []
    for h in range(acc.shape[1] // dh):
        x = acc[:, h * dh:(h + 1) * dh]
        ms = jnp.mean(x * x, axis=-1, keepdims=True)
        outs.append(x * lax.rsqrt(ms + EPS) * g)
    y = jnp.concatenate(outs, axis=1) if len(outs) > 1 else outs[0]
    return [y] * n_out


def _epi_merge(accs, extras):
    ga, gb = extras
    return [_sigmoid(ga) * accs[0] + _sigmoid(gb) * accs[1]]


def _epi_residual(accs, extras):
    return [extras[0] + accs[0]]


def _epi_swiglu(accs, extras):
    return [_silu(accs[0]) * accs[1]]


def _ret_prompt_kernel(q_ref, k_ref, v_ref, rg_ref, dec_ref, cd_ref, kd_ref, gc_ref, yr_ref, st_ref,
                       *, nsub, C):
    @pl.when(pl.program_id(1) == 0)
    def _():
        st_ref[...] = jnp.zeros_like(st_ref)

    dec = dec_ref[0]
    cd = cd_ref[0]
    kd = kd_ref[0]
    gc = gc_ref[0]
    for t in range(nsub):
        rows = slice(t * C, (t + 1) * C)
        q = q_ref[rows, :]
        k = k_ref[rows, :]
        v = v_ref[rows, :]
        st = st_ref[0]
        s = _dot_nt(q, k) * dec
        o = _dot(s.astype(BF16), v) + _dot(q, st.astype(BF16)) * cd
        kdt = (k.astype(F32) * kd).T.astype(BF16)
        st_ref[0] = gc * st + _dot(kdt, v)
        ms = jnp.mean(o * o, axis=-1, keepdims=True)
        yr_ref[rows, :] = (o * lax.rsqrt(ms + EPS) * _silu(rg_ref[rows, :])).astype(yr_ref.dtype)


def _retention_prompt(rq, rk, rv, rg, tabs):
    S = rq.shape[0]
    H, DK, DV, C = RET_HEADS, RET_DK, RET_DV, RET_CHUNK
    nsub = _pick(S // C, (4, 2, 1))
    T = nsub * C
    dec, cd, kd, gc = tabs
    kern = functools.partial(_ret_prompt_kernel, nsub=nsub, C=C)
    return pl.pallas_call(
        kern,
        grid=(H, S // T),
        in_specs=[
            pl.BlockSpec((T, DK), lambda h, c: (c, h)),
            pl.BlockSpec((T, DK), lambda h, c: (c, h)),
            pl.BlockSpec((T, DV), lambda h, c: (c, h)),
            pl.BlockSpec((T, DV), lambda h, c: (c, h)),
            pl.BlockSpec((1, C, C), lambda h, c: (h, 0, 0)),
            pl.BlockSpec((1, C, DV), lambda h, c: (h, 0, 0)),
            pl.BlockSpec((1, C, DK), lambda h, c: (h, 0, 0)),
            pl.BlockSpec((1, 1, DV), lambda h, c: (h, 0, 0)),
        ],
        out_specs=[
            pl.BlockSpec((T, DV), lambda h, c: (c, h)),
            pl.BlockSpec((1, DK, DV), lambda h, c: (h, 0, 0)),
        ],
        out_shape=[jax.ShapeDtypeStruct((S, H * DV), BF16), jax.ShapeDtypeStruct((H, DK, DV), F32)],
        compiler_params=_cparams("parallel", "arbitrary"),
        name="retention_prompt",
    )(rq, rk, rv, rg, dec, cd, kd, gc)


def _retention_tables(log_gamma):
    H, DK, DV, C = RET_HEADS, RET_DK, RET_DV, RET_CHUNK
    i = jnp.arange(C, dtype=F32)
    rel = i[:, None] - i[None, :]
    causal = rel >= 0
    dec = jnp.where(causal[None], jnp.exp(jnp.where(causal, rel, 0.0)[None] * log_gamma[:, None, None]), 0.0)
    cross = jnp.exp((i[:, None] + 1.0) * log_gamma[None, :])
    kdec = jnp.exp((C - 1.0 - i)[:, None] * log_gamma[None, :])
    gC = jnp.exp(C * log_gamma)
    cd = jnp.broadcast_to(cross.T[:, :, None], (H, C, DV))
    kd = jnp.broadcast_to(kdec.T[:, :, None], (H, C, DK))
    gc = jnp.broadcast_to(gC[:, None, None], (H, 1, DV))
    return dec, cd, kd, gc


def _ret_sample_kernel(st_ref, qt_ref, kt_ref, v_ref, rg_ref, g1_ref, o_ref, ns_ref, *, H):
    qt = qt_ref[0]
    kt = kt_ref[0]
    v = v_ref[0]
    rg = rg_ref[0]
    for h in range(H):
        st = st_ref[0, 0, h]
        qc = qt[:, h:h + 1]
        kc = kt[:, h:h + 1]
        vr = v[h:h + 1, :]
        g1 = g1_ref[h]
        cross = jnp.sum(st * qc, axis=0, keepdims=True)
        qk = jnp.sum(qc * kc, axis=0, keepdims=True)
        o = qk * vr + cross * g1
        ns_ref[0, 0, h] = g1 * st + kc * vr
        ms = jnp.mean(o * o, axis=-1, keepdims=True)
        o_ref[0, h:h + 1, :] = o * lax.rsqrt(ms + EPS) * _silu(rg[h:h + 1, :])


def _retention_sample(state_ret, layer, rq, rk, rv, rg, log_gamma):
    DB = rq.shape[0]
    H, DK, DV = RET_HEADS, RET_DK, RET_DV
    qt = rq.reshape(DB, H, DK).transpose(0, 2, 1)
    kt = rk.reshape(DB, H, DK).transpose(0, 2, 1)
    g1 = jnp.broadcast_to(jnp.exp(1.0 * log_gamma)[:, None, None], (H, 1, DV))
    kern = functools.partial(_ret_sample_kernel, H=H)
    yr, ns = pl.pallas_call(
        kern,
        grid=(DB,),
        in_specs=[
            pl.BlockSpec((1, 1, H, DK, DV), lambda b: (layer, b, 0, 0, 0)),
            pl.BlockSpec((1, DK, H), lambda b: (b, 0, 0)),
            pl.BlockSpec((1, DK, H), lambda b: (b, 0, 0)),
            pl.BlockSpec((1, H, DV), lambda b: (b, 0, 0)),
            pl.BlockSpec((1, H, DV), lambda b: (b, 0, 0)),
            pl.BlockSpec((H, 1, DV), lambda b: (0, 0, 0)),
        ],
        out_specs=[
            pl.BlockSpec((1, H, DV), lambda b: (b, 0, 0)),
            pl.BlockSpec((1, 1, H, DK, DV), lambda b: (0, b, 0, 0, 0)),
        ],
        out_shape=[jax.ShapeDtypeStruct((DB, H, DV), F32),
                   jax.ShapeDtypeStruct((1, DB, H, DK, DV), state_ret.dtype)],
        compiler_params=_cparams("parallel"),
        name="retention_sample",
    )(state_ret, qt, kt, rv.reshape(DB, H, DV), rg.reshape(DB, H, DV), g1)
    return yr.reshape(DB, H * DV), ns[0]


def _sortable_key(x):
    bits = pltpu.bitcast(x, I32)
    return bits ^ ((bits >> 31) & 0x7FFFFFFF)


def _topk_select(keys_ref, nvis, ksel, col_bits):
    _, R, CW = keys_ref.shape
    ksel_f = float(ksel)

    def count(pred):
        def body(c, acc):
            f = jnp.where(pred(keys_ref[c], c), 1.0, 0.0)
            for g in range(CW // LANES):
                acc = acc + f[:, g * LANES:(g + 1) * LANES]
            return acc
        acc = lax.fori_loop(0, nvis, body, jnp.zeros((R, LANES), F32))
        return jnp.sum(acc, axis=-1, keepdims=True)

    def value_bit(bi, u):
        cand_u = u | lax.shift_left(jnp.int32(1), 31 - bi)
        cand = cand_u ^ INT_MIN
        cnt = count(lambda key, c: key >= cand)
        return jnp.where(cnt >= ksel_f, cand_u, u)

    u = lax.fori_loop(0, 32, value_bit, jnp.zeros((R, 1), I32))
    T = u ^ INT_MIN
    need = ksel_f - count(lambda key, c: key > T)
    ties = count(lambda key, c: key == T)

    def col_of(c):
        return c * CW + lax.broadcasted_iota(I32, (R, CW), 1)

    def col_search():
        def col_bit(bi, x):
            cand = x | lax.shift_left(jnp.int32(1), col_bits - 1 - bi)
            cnt = count(lambda key, c: (key == T) & (col_of(c) < cand))
            return jnp.where(cnt < need, cand, x)
        return lax.fori_loop(0, col_bits, col_bit, jnp.zeros((R, 1), I32))

    contested = jnp.max(jnp.where((ties > need) & (T != INT_MIN), 1.0, 0.0)) > 0.0
    J = lax.cond(contested, col_search, lambda: jnp.full((R, 1), 2 ** 30, I32))
    J = jnp.where(T == INT_MIN, -1, J)
    return T, J


def _dsa_select_kernel(iq_ref, iw_ref, ik_ref, bias_ref, keys_ref, *, ksel, col_bits):
    NC, QB, CW = keys_ref.shape
    i = pl.program_id(0)
    nvis = ((i + 1) * QB + CW - 1) // CW
    w = iw_ref[...] * (IDX_HEADS ** -0.5 * IDX_DIM ** -0.5)
    rowpos = i * QB + lax.broadcasted_iota(I32, (QB, 1), 0)

    def col_of(c):
        return c * CW + lax.broadcasted_iota(I32, (QB, CW), 1)

    def score_chunk(c, carry):
        kc = ik_ref[c]
        acc = jnp.zeros((QB, CW), F32)
        for h in range(IDX_HEADS):
            s = _dot_nt(iq_ref[:, h * IDX_DIM:(h + 1) * IDX_DIM], kc)
            acc = acc + jnp.maximum(s, 0.0) * w[:, h:h + 1]
        keys_ref[c] = jnp.where(col_of(c) <= rowpos, _sortable_key(acc), INT_MIN)
        return carry

    lax.fori_loop(0, nvis, score_chunk, 0)
    T, J = _topk_select(keys_ref, nvis, ksel, col_bits)

    def emit(c, carry):
        key = keys_ref[c]
        col = col_of(c)
        sel = ((key > T) | ((key == T) & (col <= J))) & (col <= rowpos)
        bias_ref[0, c] = jnp.where(sel, 0.0, NEG_BIAS).astype(bias_ref.dtype)
        return carry

    lax.fori_loop(0, nvis, emit, 0)

    def fill(c, carry):
        bias_ref[0, c] = jnp.full((QB, CW), NEG_BIAS, bias_ref.dtype)
        return carry

    lax.fori_loop(nvis, NC, fill, 0)


def _dsa_select_prompt(iq, iw, ik, ksel, CW):
    S = iq.shape[0]
    QB = Q_BLOCK
    NQ, NC = S // QB, S // CW
    kern = functools.partial(_dsa_select_kernel, ksel=ksel, col_bits=int(np.ceil(np.log2(S))))
    return pl.pallas_call(
        kern,
        grid=(NQ,),
        in_specs=[
            pl.BlockSpec((QB, iq.shape[1]), lambda i: (i, 0)),
            pl.BlockSpec((QB, iw.shape[1]), lambda i: (i, 0)),
            pl.BlockSpec((NC, CW, ik.shape[1]), lambda i: (0, 0, 0)),
        ],
        out_specs=pl.BlockSpec((1, NC, QB, CW), lambda i: (i, 0, 0, 0)),
        out_shape=jax.ShapeDtypeStruct((NQ, NC, QB, CW), BF16),
        scratch_shapes=[pltpu.VMEM((NC, QB, CW), I32)],
        compiler_params=_cparams("parallel"),
        name="dsa_select_prompt",
    )(iq, iw, ik.reshape(NC, CW, ik.shape[1]))


def _dsa_attn_kernel(q_ref, k_ref, v_ref, b_ref, o_ref, m_sc, l_sc, acc_sc, *, H, DH, QB, TK, SUB):
    i = pl.program_id(0)
    j = pl.program_id(1)
    last = ((i + 1) * QB - 1) // TK

    @pl.when(j == 0)
    def _():
        m_sc[...] = jnp.full(m_sc.shape, NEG_BIAS, F32)
        l_sc[...] = jnp.zeros_like(l_sc)
        acc_sc[...] = jnp.zeros_like(acc_sc)

    @pl.when(j <= last)
    def _():
        for t in range(TK // SUB):
            keys = slice(t * SUB, (t + 1) * SUB)
            b = b_ref[0, 0, :, keys].astype(F32)
            for h in range(H):
                cols = slice(h * DH, (h + 1) * DH)
                s = _dot_nt(q_ref[:, cols], k_ref[keys, cols]) + b
                m_prev = m_sc[h]
                m_new = jnp.maximum(m_prev, jnp.max(s, axis=-1, keepdims=True))
                alpha = jnp.exp2(m_prev - m_new)
                p = jnp.exp2(s - jnp.concatenate([m_new] * (SUB // LANES), axis=1))
                l_sc[h] = alpha * l_sc[h] + jnp.sum(p, axis=-1, keepdims=True)
                acc_sc[:, cols] = alpha * acc_sc[:, cols] + _dot(p.astype(BF16), v_ref[keys, cols])
                m_sc[h] = m_new

    @pl.when(j == last)
    def _():
        for h in range(H):
            cols = slice(h * DH, (h + 1) * DH)
            o_ref[:, cols] = (acc_sc[:, cols] / l_sc[h]).astype(o_ref.dtype)


def _dsa_attn_prompt(aq, ak, av, bias, TK):
    S, W = aq.shape
    H, DH, QB = ATT_HEADS, ATT_DH, Q_BLOCK
    assert DH == LANES

    def kv_map(i, j):
        return (jnp.minimum(j, ((i + 1) * QB - 1) // TK), 0)

    def b_map(i, j):
        return (i, jnp.minimum(j, ((i + 1) * QB - 1) // TK), 0, 0)

    kern = functools.partial(_dsa_attn_kernel, H=H, DH=DH, QB=QB, TK=TK, SUB=_pick(TK, (256, 128)))
    return pl.pallas_call(
        kern,
        grid=(S // QB, S // TK),
        in_specs=[
            pl.BlockSpec((QB, W), lambda i, j: (i, 0)),
            pl.BlockSpec((TK, W), kv_map),
            pl.BlockSpec((TK, W), kv_map),
            pl.BlockSpec((1, 1, QB, TK), b_map),
        ],
        out_specs=pl.BlockSpec((QB, W), lambda i, j: (i, 0)),
        out_shape=jax.ShapeDtypeStruct((S, W), BF16),
        scratch_shapes=[pltpu.VMEM((H, QB, LANES), F32), pltpu.VMEM((H, QB, LANES), F32),
                        pltpu.VMEM((QB, W), F32)],
        compiler_params=_cparams("parallel", "arbitrary"),
        name="dsa_attn_prompt",
    )(aq, ak, av, bias)


def _idx_sample_kernel(pt_ref, iq_ref, w_ref, *refs):
    page_refs, o_ref = refs[:-1], refs[-1]
    iq = iq_ref[0].astype(BF16)
    w = w_ref[0]
    rows = []
    for page_ref in page_refs:
        s = _dot_nt(iq, page_ref[0, 0].astype(BF16))
        rows.append(jnp.sum(jnp.maximum(s, 0.0) * w, axis=0, keepdims=True))
    o_ref[0, 0] = jnp.concatenate(rows, axis=1)


def _idx_sample(cache_idx_k, layer, page_table, iq, iw):
    DB, NP = page_table.shape
    P, Di = cache_idx_k.shape[2:]
    Hi = IDX_HEADS
    G = _pick(NP, (16, 8, 4, 2, 1))
    w = jnp.broadcast_to((iw * (IDX_HEADS ** -0.5 * IDX_DIM ** -0.5))[:, :, None], (DB, Hi, P))

    def page_spec(r):
        return pl.BlockSpec((1, 1, P, Di), lambda b, g, pt: (layer, pt[b, g * G + r], 0, 0))

    out = pl.pallas_call(
        _idx_sample_kernel,
        grid_spec=pltpu.PrefetchScalarGridSpec(
            num_scalar_prefetch=1,
            grid=(DB, NP // G),
            in_specs=[
                pl.BlockSpec((1, Hi, Di), lambda b, g, pt: (b, 0, 0)),
                pl.BlockSpec((1, Hi, P), lambda b, g, pt: (b, 0, 0)),
            ] + [page_spec(r) for r in range(G)],
            out_specs=pl.BlockSpec((1, 1, 1, G * P), lambda b, g, pt: (b, g, 0, 0)),
        ),
        out_shape=jax.ShapeDtypeStruct((DB, NP // G, 1, G * P), F32),
        compiler_params=_cparams("parallel", "arbitrary"),
        name="dsa_index_sample",
    )(page_table, iq.reshape(DB, Hi, Di), w, *([cache_idx_k] * G))
    return out.reshape(DB, NP * P)


def _select_sample_kernel(sc_ref, iq_ref, ikn_ref, iw_ref, idx_ref, keys_ref, cnt_ref, *, ksel, past_len,
                          col_bits):
    NC, R, CW = keys_ref.shape
    NP = NC - 1
    w = iw_ref[...] * (IDX_HEADS ** -0.5 * IDX_DIM ** -0.5)
    ikn = ikn_ref[...]
    new = jnp.zeros((R, 1), F32)
    for h in range(IDX_HEADS):
        s = jnp.sum(iq_ref[:, h * IDX_DIM:(h + 1) * IDX_DIM] * ikn, axis=-1, keepdims=True)
        new = new + jnp.maximum(s, 0.0) * w[:, h:h + 1]

    def col_of(c):
        return c * CW + lax.broadcasted_iota(I32, (R, CW), 1)

    def load(c, carry):
        keys_ref[c] = _sortable_key(sc_ref[c] + 0.0)
        return carry

    lax.fori_loop(0, NP, load, 0)
    keys_ref[NP] = jnp.where(col_of(NP) <= past_len, _sortable_key(jnp.broadcast_to(new, (R, CW))), INT_MIN)
    T, J = _topk_select(keys_ref, NC, ksel, col_bits)

    tri = (lax.broadcasted_iota(I32, (CW, CW), 0) <= lax.broadcasted_iota(I32, (CW, CW), 1)).astype(BF16)

    def prefix(c, before):
        key = keys_ref[c]
        col = col_of(c)
        sel = ((key > T) | ((key == T) & (col <= J))) & (col <= past_len)
        f = jnp.where(sel, 1.0, 0.0)
        cnt_ref[c] = _dot(f.astype(BF16), tri) + before
        return before + jnp.sum(f, axis=-1, keepdims=True)

    lax.fori_loop(0, NC, prefix, jnp.zeros((R, 1), F32))

    jcol = lax.broadcasted_iota(I32, (ksel, 1), 0).astype(F32)
    lane = lax.broadcasted_iota(I32, (ksel, LANES), 1)

    def one_row(r, out):
        def chunk(c, acc):
            return acc + jnp.where(cnt_ref[c, pl.ds(r, 1), :] <= jcol, 1.0, 0.0)
        acc = lax.fori_loop(0, NC, chunk, jnp.zeros((ksel, CW), F32))
        return jnp.where(lane == r, jnp.sum(acc, axis=-1, keepdims=True), out)

    idx_ref[...] = lax.fori_loop(0, R, one_row, jnp.zeros((ksel, LANES), F32)).astype(I32)


def _select_sample(scores, iq, ik_new, iw, ksel):
    DB = scores.shape[0]
    P = LANES
    NP = scores.shape[1] // P
    past_len = NP * P
    assert DB <= LANES
    sc = scores.reshape(DB, NP, P).transpose(1, 0, 2)
    kern = functools.partial(_select_sample_kernel, ksel=ksel, past_len=past_len,
                             col_bits=int(np.ceil(np.log2(past_len + P))))
    idx_t = pl.pallas_call(
        kern,
        out_shape=jax.ShapeDtypeStruct((ksel, LANES), I32),
        scratch_shapes=[pltpu.VMEM((NP + 1, DB, P), I32), pltpu.VMEM((NP + 1, DB, P), F32)],
        compiler_params=pltpu.CompilerParams(vmem_limit_bytes=VMEM_LIMIT_BYTES),
        name="dsa_select_sample",
    )(sc, iq, ik_new, iw)
    return idx_t[:, :DB].T


def _attn_sample_kernel(idx_ref, pt_ref, q_ref, kn_ref, vn_ref, *refs, G, ksel, past_len):
    k_refs, v_refs = refs[:G], refs[G:2 * G]
    o_ref, m_sc, l_sc, acc_sc = refs[2 * G:]
    b = pl.program_id(0)
    g = pl.program_id(1)
    scale = q_ref.shape[-1] ** -0.5

    @pl.when(g == 0)
    def _():
        m_sc[...] = jnp.full(m_sc.shape, NEG_BIAS, F32)
        l_sc[...] = jnp.zeros_like(l_sc)
        acc_sc[...] = jnp.zeros_like(acc_sc)

    q = q_ref[0]

    def slot_bias(r):
        pos = idx_ref[b * ksel + g * G + r]
        return jnp.full((1, 1, 1), jnp.where(pos >= past_len, NEG_BIAS, 0.0), F32)

    def update(kk, vv, bias3):
        s = jnp.sum(kk * q[None], axis=-1, keepdims=True) * scale + bias3
        m_prev = m_sc[...]
        m_new = jnp.maximum(m_prev, jnp.max(s, axis=0))
        alpha = jnp.exp(m_prev - m_new)
        pr = jnp.exp(s - m_new[None])
        l_sc[...] = alpha * l_sc[...] + jnp.sum(pr, axis=0)
        acc_sc[...] = alpha * acc_sc[...] + jnp.sum(pr * vv, axis=0)
        m_sc[...] = m_new

    kk = jnp.concatenate([r[0, 0] for r in k_refs], axis=0)
    vv = jnp.concatenate([r[0, 0] for r in v_refs], axis=0)
    update(kk, vv, jnp.concatenate([slot_bias(r) for r in range(G)], axis=0))

    @pl.when(g == pl.num_programs(1) - 1)
    def _():
        last = idx_ref[b * ksel + ksel - 1]
        bnew = jnp.full((1, 1, 1), jnp.where(last >= past_len, 0.0, NEG_BIAS), F32)
        update(kn_ref[...], vn_ref[...], bnew)
        o_ref[0] = acc_sc[...] / l_sc[...]


def _attn_sample(cache_k, cache_v, layer, page_table, aq, ak_new, av_new, idx):
    DB, NP = page_table.shape
    P, H, DH = cache_k.shape[2:]
    ksel = idx.shape[1]
    past_len = NP * P
    G = _pick(ksel, (32, 16, 8, 4, 2, 1))

    def row_spec(r):
        def index_map(b, g, idx_ref, pt):
            pos = jnp.minimum(idx_ref[b * ksel + g * G + r], past_len - 1)
            return (layer, pt[b, pos // P], pos % P, 0, 0)
        return pl.BlockSpec((1, 1, 1, H, DH), index_map)

    vec_spec = pl.BlockSpec((1, H, DH), lambda b, g, idx_ref, pt: (b, 0, 0))
    kern = functools.partial(_attn_sample_kernel, G=G, ksel=ksel, past_len=past_len)
    out = pl.pallas_call(
        kern,
        grid_spec=pltpu.PrefetchScalarGridSpec(
            num_scalar_prefetch=2,
            grid=(DB, ksel // G),
            in_specs=[vec_spec, vec_spec, vec_spec] + [row_spec(r) for r in range(G)] * 2,
            out_specs=vec_spec,
            scratch_shapes=[pltpu.VMEM((H, 1), F32), pltpu.VMEM((H, 1), F32), pltpu.VMEM((H, DH), F32)],
        ),
        out_shape=jax.ShapeDtypeStruct((DB, H, DH), F32),
        compiler_params=_cparams("parallel", "arbitrary"),
        name="dsa_attn_sample",
    )(idx.reshape(-1), page_table, aq.reshape(DB, H, DH), ak_new.reshape(DB, H, DH), av_new.reshape(DB, H, DH),
      *([cache_k] * G), *([cache_v] * G))
    return out.reshape(DB, H * DH)


def _rotary_tables(pos):
    d = RET_DK
    inv_freq = 1.0 / (ROPE_BASE ** jnp.linspace(0.0, 1.0, d // 2, dtype=F32))
    ang = pos.astype(F32)[:, None] * inv_freq[None, :]
    cos = jnp.repeat(jnp.cos(ang), 2, axis=1)
    sin = jnp.sin(ang)
    sin_signed = jnp.stack([-sin, sin], axis=-1).reshape(pos.shape[0], d)
    return cos, sin_signed


def _in_proj_columns(w_in):
    rqk, rv_w = RET_HEADS * RET_DK, RET_HEADS * RET_DV
    aw, iqw = ATT_HEADS * ATT_DH, IDX_HEADS * IDX_DIM
    d_model = w_in.shape[1]
    names = ("rq", "rk", "rv", "rg", "aq", "ak", "av", "iq", "ik", "iw", "ga", "gb")
    sizes = (rqk, rqk, rv_w, rv_w, aw, aw, aw, iqw, IDX_DIM, IDX_HEADS, d_model, d_model)
    offs = np.concatenate([[0], np.cumsum(sizes)]).tolist()
    assert offs[-1] == w_in.shape[2]
    return {n: (offs[t], sizes[t]) for t, n in enumerate(names)}


def _unaligned_weights(w_in, layer, cols):
    w = w_in[layer]
    out = {n: w[:, cols[n][0]:cols[n][0] + cols[n][1]].astype(BF16) for n in ("ga", "gb")}
    ikw = w[:, cols["ik"][0]:cols["iw"][0] + cols["iw"][1]]
    out["ikw"] = jnp.pad(ikw, ((0, 0), (0, 2 * LANES - ikw.shape[1]))).astype(BF16)
    return out


def _mix_inputs(h, w_in, layer, cols, wu, qg, kg, cos, sin_signed, act_dt, q_scale):
    M = h.shape[0]
    tm = _pick(M, (512, 256, 128, 64, 32, 16, 8))

    def proj(name, epi, extras, dts, tn_prefs=(512, 256, 128)):
        if name in wu:
            b = (0, wu[name], None, 0, wu[name].shape[1])
            tn = _pick(b[4], tn_prefs)
        else:
            col0, n = cols[name]
            b = (0, w_in, layer, col0, n)
            tn = _pick(int(np.gcd(col0, n)), tn_prefs)
        return _mm([h], [b], extras, epi, dts, tm=tm, tn=tn, name="proj_" + name)

    rot = [(cos, "rows"), (sin_signed, "rows")]
    (rq,) = proj("rq", functools.partial(_epi_rotary, 1.0), rot, [act_dt])
    (rk,) = proj("rk", functools.partial(_epi_rotary, RET_DK ** -0.5), rot, [act_dt])
    (rv,) = proj("rv", _epi_identity, [], [act_dt])
    (rg,) = proj("rg", _epi_identity, [], [F32])
    (aq,) = proj("aq", functools.partial(_epi_head_norm, 1), [(qg.reshape(1, -1) * q_scale, "full")], [act_dt])
    ak, ak16 = proj("ak", functools.partial(_epi_head_norm, 2), [(kg.reshape(1, -1), "full")], [F32, BF16])
    av, av16 = proj("av", _epi_two_copies, [], [F32, BF16])
    (iq,) = proj("iq", _epi_identity, [], [act_dt])
    (ikw,) = proj("ikw", _epi_identity, [], [F32], tn_prefs=(256,))
    (ga,) = proj("ga", _epi_identity, [], [F32])
    (gb,) = proj("gb", _epi_identity, [], [F32])
    ik = ikw[:, :IDX_DIM]
    iw = ikw[:, IDX_DIM:IDX_DIM + IDX_HEADS]
    return rq, rk, rv, rg, aq, ak, ak16, av, av16, iq, ik, iw, ga, gb


def _merge_ffn(x, yr, ya, ga, gb, layer, w_ret_o, w_att_o, w_out, g_ffn, w_gate, w_up, wd):
    M, D = x.shape
    F = w_gate.shape[2]
    tm = _pick(M, (512, 256, 128, 64, 32, 16, 8))
    tn = _pick(D, (512, 256, 128))
    (m,) = _mm([yr, ya], [(0, w_ret_o, layer, 0, D), (1, w_att_o, layer, 0, D)], [(ga, "tile"), (gb, "tile")],
               _epi_merge, [BF16], tm=tm, tn=tn, name="merge")
    (x1,) = _mm([m], [(0, w_out, layer, 0, D)], [(x, "tile")], _epi_residual, [F32], tm=tm, tn=tn,
                name="out_proj")
    hf = _rmsnorm(x1, g_ffn)
    (u,) = _mm([hf], [(0, w_gate, layer, 0, F), (0, w_up, layer, 0, F)], [], _epi_swiglu, [BF16], tm=tm,
               tn=_pick(F, (256, 128)), name="ffn_up")
    (y,) = _mm([u], [(0, wd, None, 0, D)], [(x1, "tile")], _epi_residual, [F32], tm=tm, tn=_pick(D, (256, 128)),
               name="ffn_down", hold_b=False)
    return y


def kernel(x_prompt, x_sample, cache_k, cache_v, cache_idx_k, state_ret, page_table, norm_mix_g, w_in,
           q_norm_g, k_norm_g, w_ret_o, w_att_o, w_out, norm_ffn_g, w_ffn_gate, w_ffn_up, w_ffn_down):
    depth = w_in.shape[0]
    B, S, D = x_prompt.shape
    DB, T, _ = x_sample.shape
    assert B == 1 and T == 1
    page_size = cache_k.shape[2]
    past_len = page_table.shape[1] * page_size

    log_gamma = jnp.log1p(-jnp.exp2(-5.0 - jnp.arange(RET_HEADS, dtype=F32)))
    ret_tabs = _retention_tables(log_gamma)
    cos_p, sin_p = _rotary_tables(jnp.arange(S, dtype=I32))
    cos_s, sin_s = _rotary_tables(jnp.full((DB,), past_len, I32))
    ksel_p = min(TOPK_MAX, S // 4)
    ksel_s = min(TOPK_MAX, (past_len + T) // 4)
    cw = _pick(S, (512, 256, 128))

    xp = x_prompt.reshape(S, D)
    xs = x_sample.reshape(DB, D)
    kp, vp, ikp, stp, ksm, vsm, iks, sts = [], [], [], [], [], [], [], []
    for layer in range(depth):
        cols = _in_proj_columns(w_in)
        wun = _unaligned_weights(w_in, layer, cols)
        wd = w_ffn_down[layer].astype(BF16)
        dense = (layer, w_ret_o, w_att_o, w_out, norm_ffn_g[layer], w_ffn_gate, w_ffn_up, wd)

        h = _rmsnorm(xp, norm_mix_g[layer])
        rq, rk, rv, rg, aq, ak, ak16, av, av16, iq, ik, iw, ga, gb = _mix_inputs(
            h, w_in, layer, cols, wun, q_norm_g[layer], k_norm_g[layer], cos_p, sin_p, BF16,
            ATT_DH ** -0.5 * float(np.log2(np.e)))
        yr, st_p = _retention_prompt(rq, rk, rv, rg, ret_tabs)
        bias = _dsa_select_prompt(iq, iw, ik.astype(BF16), ksel_p, cw)
        ya = _dsa_attn_prompt(aq, ak16, av16, bias, cw)
        xp = _merge_ffn(xp, yr, ya, ga, gb, *dense)
        kp.append(ak.reshape(B, S, ATT_HEADS, ATT_DH))
        vp.append(av.reshape(B, S, ATT_HEADS, ATT_DH))
        ikp.append(ik.reshape(B, S, IDX_DIM))
        stp.append(st_p[None].astype(state_ret.dtype))

        h = _rmsnorm(xs, norm_mix_g[layer])
        rq, rk, rv, rg, aq, ak, ak16, av, av16, iq, ik, iw, ga, gb = _mix_inputs(
            h, w_in, layer, cols, wun, q_norm_g[layer], k_norm_g[layer], cos_s, sin_s, F32, 1.0)
        yr, st_s = _retention_sample(state_ret, layer, rq, rk, rv, rg, log_gamma)
        scores = _idx_sample(cache_idx_k, layer, page_table, iq, iw)
        sel_idx = _select_sample(scores, iq, ik, iw, ksel_s)
        ya = _attn_sample(cache_k, cache_v, layer, page_table, aq, ak, av, sel_idx)
        xs = _merge_ffn(xs, yr.astype(BF16), ya.astype(BF16), ga, gb, *dense)
        ksm.append(ak.reshape(DB, T, ATT_HEADS, ATT_DH))
        vsm.append(av.reshape(DB, T, ATT_HEADS, ATT_DH))
        iks.append(ik.reshape(DB, T, IDX_DIM))
        sts.append(st_s)
    return (xp.reshape(B, S, D), xs.reshape(DB, T, D), jnp.stack(kp), jnp.stack(vp), jnp.stack(ikp),
            jnp.stack(stp), jnp.stack(ksm), jnp.stack(vsm), jnp.stack(iks), jnp.stack(sts))
```

```python
import functools

import numpy as np
import jax
import jax.numpy as jnp
from jax import lax
from jax.experimental import pallas as pl
from jax.experimental.pallas import tpu as pltpu

F32 = jnp.float32
BF16 = jnp.bfloat16
I32 = jnp.int32

RET_HEADS = 8
RET_DK = 256
RET_DV = 512
RET_CHUNK = 128
ROPE_BASE = 10000.0
ATT_HEADS = 16
ATT_DH = 128
IDX_HEADS = 8
IDX_DIM = 128
TOPK_MAX = 256
Q_BLOCK = 128
EPS = 1e-6

LANES = 128
VMEM_LIMIT_BYTES = 56 * 1024 * 1024

ROW_TILES = (1024, 512, 256, 128, 64, 32, 16, 8)

NEG_BIAS = -1e30
INT_MIN = -(2 ** 31)


def _cparams(*sem):
    return pltpu.CompilerParams(dimension_semantics=sem, vmem_limit_bytes=VMEM_LIMIT_BYTES)


def _pick(n, prefs):
    for p in prefs:
        if p <= n and n % p == 0:
            return p
    return n


def _dot(a, b):
    return jnp.dot(a, b, preferred_element_type=F32)


def _dot_nt(a, b):
    return lax.dot_general(a, b, (((1,), (1,)), ((), ())), preferred_element_type=F32)


def _sigmoid(x):
    return 1.0 / (1.0 + jnp.exp(-x))


def _silu(x):
    return x * _sigmoid(x)


def _rmsnorm_kernel(x_ref, g_ref, o_ref):
    x = x_ref[...]
    ms = jnp.mean(x * x, axis=-1, keepdims=True)
    o_ref[...] = (x * lax.rsqrt(ms + EPS) * g_ref[...]).astype(o_ref.dtype)


def _rmsnorm(x, g):
    M, D = x.shape
    tm = _pick(M, (256, 128, 64, 32, 16, 8))
    return pl.pallas_call(
        _rmsnorm_kernel,
        grid=(M // tm,),
        in_specs=[pl.BlockSpec((tm, D), lambda i: (i, 0)), pl.BlockSpec((1, D), lambda i: (0, 0))],
        out_specs=pl.BlockSpec((tm, D), lambda i: (i, 0)),
        out_shape=jax.ShapeDtypeStruct((M, D), BF16),
        compiler_params=_cparams("parallel"),
        name="rmsnorm",
    )(x, g.reshape(1, D))


def _mm_kernel(*refs, n_a, b_src, b_nt, n_e, n_o, cast, epilogue):
    n_b = len(b_src)
    a_refs = refs[:n_a]
    b_refs = refs[n_a:n_a + n_b]
    e_refs = refs[n_a + n_b:n_a + n_b + n_e]
    o_refs = refs[n_a + n_b + n_e:n_a + n_b + n_e + n_o]
    if cast:
        w_refs = refs[n_a + n_b + n_e + n_o:]

        @pl.when(pl.program_id(1) == 0)
        def _():
            for b, w in zip(b_refs, w_refs):
                w[...] = b[...].reshape(w.shape).astype(w.dtype)
        b_refs = w_refs
    accs = [(_dot_nt if nt else _dot)(a_refs[s][...], b[...]) for s, nt, b in zip(b_src, b_nt, b_refs)]
    res = epilogue(accs, [e[...] for e in e_refs])
    for o, r in zip(o_refs, res):
        o[...] = r.astype(o.dtype)


def _mm(a_list, b_list, extras, epilogue, out_dtypes, *, tm, tn, name, hold_b=True):
    M = a_list[0].shape[0]
    N = b_list[0][4]
    cast = b_list[0][2] is not None
    assert all((b[2] is not None) == cast for b in b_list) and (hold_b or not cast)
    if hold_b:
        grid = (N // tn, M // tm)
        ij = lambda f: (lambda j, i: f(i, j))
    else:
        grid = (M // tm, N // tn)
        ij = lambda f: f
    in_specs, args, scratch = [], [], []
    for a in a_list:
        in_specs.append(pl.BlockSpec((tm, a.shape[1]), ij(lambda i, j: (i, 0))))
        args.append(a)
    for _, w, layer, col0, _, nt in b_list:
        if nt:
            K = w.shape[2]
            if col0 % tn == 0:
                spec = pl.BlockSpec((None, tn, K), ij(lambda i, j, l=layer, c=col0 // tn: (l, c + j, 0)))
            else:
                spec = pl.BlockSpec((pl.Element(1), pl.Element(tn), pl.Element(K)),
                                    ij(lambda i, j, l=layer, c=col0: (l, pl.multiple_of(c + j * tn, 8), 0)))
            scratch.append(pltpu.VMEM((tn, K), BF16))
        elif cast:
            assert col0 % tn == 0
            K = w.shape[1]
            spec = pl.BlockSpec((None, K, tn), ij(lambda i, j, l=layer, c=col0 // tn: (l, 0, c + j)))
            scratch.append(pltpu.VMEM((K, tn), BF16))
        else:
            assert col0 % tn == 0
            spec = pl.BlockSpec((w.shape[0], tn), ij(lambda i, j, c=col0 // tn: (0, c + j)))
        in_specs.append(spec)
        args.append(w)
    for arr, kind in extras:
        if kind == "tile":
            in_specs.append(pl.BlockSpec((tm, tn), ij(lambda i, j: (i, j))))
        elif kind == "rows":
            in_specs.append(pl.BlockSpec((tm, arr.shape[1]), ij(lambda i, j: (i, 0))))
        else:
            in_specs.append(pl.BlockSpec(arr.shape, ij(lambda i, j, nd=arr.ndim: (0,) * nd)))
        args.append(arr)
    out_specs, out_shape = [], []
    for od in out_dtypes:
        if isinstance(od, tuple):
            out_specs.append(pl.BlockSpec((tn, tm), ij(lambda i, j: (j, i))))
            out_shape.append(jax.ShapeDtypeStruct((N, M), od[0]))
        else:
            out_specs.append(pl.BlockSpec((tm, tn), ij(lambda i, j: (i, j))))
            out_shape.append(jax.ShapeDtypeStruct((M, N), od))
    kern = functools.partial(_mm_kernel, n_a=len(a_list), b_src=tuple(b[0] for b in b_list),
                             b_nt=tuple(b[5] for b in b_list), n_e=len(extras), n_o=len(out_dtypes), cast=cast,
                             epilogue=epilogue)
    outs = pl.pallas_call(
        kern,
        grid=grid,
        in_specs=in_specs,
        out_specs=out_specs,
        out_shape=out_shape,
        scratch_shapes=scratch,
        compiler_params=_cparams("parallel", "arbitrary"),
        name=name,
    )(*args)
    return outs


def _epi_identity(accs, extras):
    return [accs[0]]


def _epi_two_copies(accs, extras):
    return [accs[0], accs[0]]


def _epi_rotary(scale, accs, extras):
    acc = accs[0]
    cos, sin_signed = extras
    dk = cos.shape[1]
    lane = lax.broadcasted_iota(I32, (acc.shape[0], dk), 1)
    even = (lane % 2) == 0
    outs = []
    for h in range(acc.shape[1] // dk):
        x = acc[:, h * dk:(h + 1) * dk]
        swapped = jnp.where(even, pltpu.roll(x, dk - 1, 1), pltpu.roll(x, 1, 1))
        outs.append((x * cos + swapped * sin_signed) * scale)
    return [jnp.concatenate(outs, axis=1) if len(outs) > 1 else outs[0]]


def _epi_head_norm(n_out, accs, extras):
    acc = accs[0]
    g = extras[0]
    dh = g.shape[1]
    outs = []
    for h in range(acc.shape[1] // dh):
        x = acc[:, h * dh:(h + 1) * dh]
        ms = jnp.mean(x * x, axis=-1, keepdims=True)
        outs.append(x * lax.rsqrt(ms + EPS) * g)
    y = jnp.concatenate(outs, axis=1) if len(outs) > 1 else outs[0]
    return [y] * n_out


def _epi_merge(accs, extras):
    ga, gb = extras
    return [_sigmoid(ga) * accs[0] + _sigmoid(gb) * accs[1]]


def _epi_residual(accs, extras):
    return [extras[0] + accs[0]]


def _epi_swiglu(accs, extras):
    return [_silu(accs[0]) * accs[1]]


def _ret_prompt_kernel(q_ref, k_ref, v_ref, rg_ref, dec_ref, cd_ref, kd_ref, gc_ref, yr_ref, st_ref,
                       *, nsub, C):
    @pl.when(pl.program_id(1) == 0)
    def _():
        st_ref[...] = jnp.zeros_like(st_ref)

    dec = dec_ref[0]
    cd = cd_ref[0]
    kd = kd_ref[0]
    gc = gc_ref[0]
    for t in range(nsub):
        rows = slice(t * C, (t + 1) * C)
        q = q_ref[rows, :]
        k = k_ref[rows, :]
        v = v_ref[rows, :]
        st = st_ref[0]
        s = _dot_nt(q, k) * dec
        o = _dot(s.astype(BF16), v) + _dot(q, st.astype(BF16)) * cd
        kdt = (k.astype(F32) * kd).T.astype(BF16)
        st_ref[0] = gc * st + _dot(kdt, v)
        ms = jnp.mean(o * o, axis=-1, keepdims=True)
        yr_ref[rows, :] = (o * lax.rsqrt(ms + EPS) * _silu(rg_ref[rows, :])).astype(yr_ref.dtype)


def _retention_prompt(rq, rk, rv, rg, tabs):
    S = rq.shape[0]
    H, DK, DV, C = RET_HEADS, RET_DK, RET_DV, RET_CHUNK
    nsub = _pick(S // C, (4, 2, 1))
    T = nsub * C
    dec, cd, kd, gc = tabs
    kern = functools.partial(_ret_prompt_kernel, nsub=nsub, C=C)
    return pl.pallas_call(
        kern,
        grid=(H, S // T),
        in_specs=[
            pl.BlockSpec((T, DK), lambda h, c: (c, h)),
            pl.BlockSpec((T, DK), lambda h, c: (c, h)),
            pl.BlockSpec((T, DV), lambda h, c: (c, h)),
            pl.BlockSpec((T, DV), lambda h, c: (c, h)),
            pl.BlockSpec((1, C, C), lambda h, c: (h, 0, 0)),
            pl.BlockSpec((1, C, DV), lambda h, c: (h, 0, 0)),
            pl.BlockSpec((1, C, DK), lambda h, c: (h, 0, 0)),
            pl.BlockSpec((1, 1, DV), lambda h, c: (h, 0, 0)),
        ],
        out_specs=[
            pl.BlockSpec((T, DV), lambda h, c: (c, h)),
            pl.BlockSpec((1, DK, DV), lambda h, c: (h, 0, 0)),
        ],
        out_shape=[jax.ShapeDtypeStruct((S, H * DV), BF16), jax.ShapeDtypeStruct((H, DK, DV), F32)],
        compiler_params=_cparams("parallel", "arbitrary"),
        name="retention_prompt",
    )(rq, rk, rv, rg, dec, cd, kd, gc)


def _retention_tables(log_gamma):
    H, DK, DV, C = RET_HEADS, RET_DK, RET_DV, RET_CHUNK
    i = jnp.arange(C, dtype=F32)
    rel = i[:, None] - i[None, :]
    causal = rel >= 0
    dec = jnp.where(causal[None], jnp.exp(jnp.where(causal, rel, 0.0)[None] * log_gamma[:, None, None]), 0.0)
    cross = jnp.exp((i[:, None] + 1.0) * log_gamma[None, :])
    kdec = jnp.exp((C - 1.0 - i)[:, None] * log_gamma[None, :])
    gC = jnp.exp(C * log_gamma)
    cd = jnp.broadcast_to(cross.T[:, :, None], (H, C, DV))
    kd = jnp.broadcast_to(kdec.T[:, :, None], (H, C, DK))
    gc = jnp.broadcast_to(gC[:, None, None], (H, 1, DV))
    return dec, cd, kd, gc


def _ret_sample_kernel(st_ref, qt_ref, kt_ref, v_ref, rg_ref, g1_ref, o_ref, ns_ref, *, H):
    qt = qt_ref[0]
    kt = kt_ref[0]
    v = v_ref[0]
    rg = rg_ref[0]
    for h in range(H):
        st = st_ref[0, 0, h]
        qc = qt[:, h:h + 1]
        kc = kt[:, h:h + 1]
        vr = v[h:h + 1, :]
        g1 = g1_ref[h]
        cross = jnp.sum(st * qc, axis=0, keepdims=True)
        qk = jnp.sum(qc * kc, axis=0, keepdims=True)
        o = qk * vr + cross * g1
        ns_ref[0, 0, h] = g1 * st + kc * vr
        ms = jnp.mean(o * o, axis=-1, keepdims=True)
        o_ref[0, h:h + 1, :] = o * lax.rsqrt(ms + EPS) * _silu(rg[h:h + 1, :])


def _retention_sample(state_ret, layer, rq, rk, rv, rg, log_gamma):
    DB = rq.shape[0]
    H, DK, DV = RET_HEADS, RET_DK, RET_DV
    qt = rq.reshape(DB, H, DK).transpose(0, 2, 1)
    kt = rk.reshape(DB, H, DK).transpose(0, 2, 1)
    g1 = jnp.broadcast_to(jnp.exp(1.0 * log_gamma)[:, None, None], (H, 1, DV))
    kern = functools.partial(_ret_sample_kernel, H=H)
    yr, ns = pl.pallas_call(
        kern,
        grid=(DB,),
        in_specs=[
            pl.BlockSpec((1, 1, H, DK, DV), lambda b: (layer, b, 0, 0, 0)),
            pl.BlockSpec((1, DK, H), lambda b: (b, 0, 0)),
            pl.BlockSpec((1, DK, H), lambda b: (b, 0, 0)),
            pl.BlockSpec((1, H, DV), lambda b: (b, 0, 0)),
            pl.BlockSpec((1, H, DV), lambda b: (b, 0, 0)),
            pl.BlockSpec((H, 1, DV), lambda b: (0, 0, 0)),
        ],
        out_specs=[
            pl.BlockSpec((1, H, DV), lambda b: (b, 0, 0)),
            pl.BlockSpec((1, 1, H, DK, DV), lambda b: (0, b, 0, 0, 0)),
        ],
        out_shape=[jax.ShapeDtypeStruct((DB, H, DV), F32),
                   jax.ShapeDtypeStruct((1, DB, H, DK, DV), state_ret.dtype)],
        compiler_params=_cparams("parallel"),
        name="retention_sample",
    )(state_ret, qt, kt, rv.reshape(DB, H, DV), rg.reshape(DB, H, DV), g1)
    return yr.reshape(DB, H * DV), ns[0]


def _sortable_key(x):
    bits = pltpu.bitcast(x, I32)
    return bits ^ ((bits >> 31) & 0x7FFFFFFF)


def _topk_select(keys_ref, nvis, ksel, col_bits, rows_on_lanes=False):
    if rows_on_lanes:
        _, CW, R = keys_ref.shape
        vec, col_axis = (1, R), 0
    else:
        _, R, CW = keys_ref.shape
        vec, col_axis = (R, 1), 1
    ksel_f = float(ksel)

    def count(pred):
        def body(c, acc):
            f = jnp.where(pred(keys_ref[c], c), 1.0, 0.0)
            if rows_on_lanes:
                return acc + jnp.sum(f.reshape(CW // 8, 8, R), axis=0)
            for g in range(CW // LANES):
                acc = acc + f[:, g * LANES:(g + 1) * LANES]
            return acc
        acc = lax.fori_loop(0, nvis, body, jnp.zeros((8, R) if rows_on_lanes else (R, LANES), F32))
        return jnp.sum(acc, axis=col_axis, keepdims=True)

    def value_bit(bi, u):
        cand_u = u | lax.shift_left(jnp.int32(1), 31 - bi)
        cand = cand_u ^ INT_MIN
        cnt = count(lambda key, c: key >= cand)
        return jnp.where(cnt >= ksel_f, cand_u, u)

    u = lax.fori_loop(0, 32, value_bit, jnp.zeros(vec, I32))
    T = u ^ INT_MIN
    need = ksel_f - count(lambda key, c: key > T)
    ties = count(lambda key, c: key == T)

    def col_of(c):
        return c * CW + lax.broadcasted_iota(I32, keys_ref.shape[1:], col_axis)

    def col_search():
        def col_bit(bi, x):
            cand = x | lax.shift_left(jnp.int32(1), col_bits - 1 - bi)
            cnt = count(lambda key, c: (key == T) & (col_of(c) < cand))
            return jnp.where(cnt < need, cand, x)
        return lax.fori_loop(0, col_bits, col_bit, jnp.zeros(vec, I32))

    contested = jnp.max(jnp.where((ties > need) & (T != INT_MIN), 1.0, 0.0)) > 0.0
    J = lax.cond(contested, col_search, lambda: jnp.full(vec, 2 ** 30, I32))
    J = jnp.where(T == INT_MIN, -1, J)
    return T, J


def _dsa_select_kernel(iq_ref, iw_ref, ik_ref, bias_ref, keys_ref, *, ksel, col_bits):
    NC, QB, CW = keys_ref.shape
    i = pl.program_id(0)
    nvis = ((i + 1) * QB + CW - 1) // CW
    w = iw_ref[...] * (IDX_HEADS ** -0.5 * IDX_DIM ** -0.5)
    rowpos = i * QB + lax.broadcasted_iota(I32, (QB, 1), 0)

    def col_of(c):
        return c * CW + lax.broadcasted_iota(I32, (QB, CW), 1)

    def score_chunk(c, carry):
        kc = ik_ref[c]
        acc = jnp.zeros((QB, CW), F32)
        for h in range(IDX_HEADS):
            s = _dot_nt(iq_ref[:, h * IDX_DIM:(h + 1) * IDX_DIM], kc)
            acc = acc + jnp.maximum(s, 0.0) * w[:, h:h + 1]
        keys_ref[c] = jnp.where(col_of(c) <= rowpos, _sortable_key(acc), INT_MIN)
        return carry

    lax.fori_loop(0, nvis, score_chunk, 0)
    T, J = _topk_select(keys_ref, nvis, ksel, col_bits)

    def emit(c, carry):
        key = keys_ref[c]
        col = col_of(c)
        sel = ((key > T) | ((key == T) & (col <= J))) & (col <= rowpos)
        bias_ref[0, c] = jnp.where(sel, 0.0, NEG_BIAS).astype(bias_ref.dtype)
        return carry

    lax.fori_loop(0, nvis, emit, 0)

    def fill(c, carry):
        bias_ref[0, c] = jnp.full((QB, CW), NEG_BIAS, bias_ref.dtype)
        return carry

    lax.fori_loop(nvis, NC, fill, 0)


def _dsa_select_prompt(iq, iw, ik, ksel, QB, CW):
    S = iq.shape[0]
    NQ, NC = S // QB, S // CW
    kern = functools.partial(_dsa_select_kernel, ksel=ksel, col_bits=int(np.ceil(np.log2(S))))
    return pl.pallas_call(
        kern,
        grid=(NQ,),
        in_specs=[
            pl.BlockSpec((QB, iq.shape[1]), lambda i: (i, 0)),
            pl.BlockSpec((QB, iw.shape[1]), lambda i: (i, 0)),
            pl.BlockSpec((NC, CW, ik.shape[1]), lambda i: (0, 0, 0)),
        ],
        out_specs=pl.BlockSpec((1, NC, QB, CW), lambda i: (i, 0, 0, 0)),
        out_shape=jax.ShapeDtypeStruct((NQ, NC, QB, CW), BF16),
        scratch_shapes=[pltpu.VMEM((NC, QB, CW), I32)],
        compiler_params=_cparams("parallel"),
        name="dsa_select_prompt",
    )(iq, iw, ik.reshape(NC, CW, ik.shape[1]))


def _dsa_attn_kernel(qi_ref, kj_ref, fin_ref, q_ref, k_ref, v_ref, b_ref, o_ref, m_sc, l_sc, acc_sc,
                     *, H, DH, TK, SUB):
    step = pl.program_id(0)

    @pl.when(kj_ref[step] == 0)
    def _():
        m_sc[...] = jnp.full(m_sc.shape, NEG_BIAS, F32)
        l_sc[...] = jnp.zeros_like(l_sc)
        acc_sc[...] = jnp.zeros_like(acc_sc)

    for t in range(TK // SUB):
        keys = slice(t * SUB, (t + 1) * SUB)
        b = b_ref[0, 0, :, keys].astype(F32)
        for h in range(H):
            cols = slice(h * DH, (h + 1) * DH)
            s = _dot_nt(q_ref[:, cols], k_ref[keys, cols]) + b
            m_prev = m_sc[h]
            m_new = jnp.maximum(m_prev, jnp.max(s, axis=-1, keepdims=True))
            alpha = jnp.exp2(m_prev - m_new)
            p = jnp.exp2(s - jnp.concatenate([m_new] * (SUB // LANES), axis=1))
            l_sc[h] = alpha * l_sc[h] + jnp.sum(p, axis=-1, keepdims=True)
            acc_sc[:, cols] = alpha * acc_sc[:, cols] + _dot(p.astype(BF16), v_ref[keys, cols])
            m_sc[h] = m_new

    @pl.when(fin_ref[step] == 1)
    def _():
        for h in range(H):
            cols = slice(h * DH, (h + 1) * DH)
            o_ref[:, cols] = (acc_sc[:, cols] / l_sc[h]).astype(o_ref.dtype)


def _dsa_attn_prompt(aq, ak, av, bias, QB, TK):
    S, W = aq.shape
    H, DH = ATT_HEADS, ATT_DH
    assert DH == LANES
    pairs = [(i, j) for i in range(S // QB) for j in range(((i + 1) * QB - 1) // TK + 1)]
    qi = jnp.asarray([p[0] for p in pairs], I32)
    kj = jnp.asarray([p[1] for p in pairs], I32)
    fin = jnp.asarray([int(p[1] == ((p[0] + 1) * QB - 1) // TK) for p in pairs], I32)
    kern = functools.partial(_dsa_attn_kernel, H=H, DH=DH, TK=TK, SUB=_pick(TK, (256, 128)))
    return pl.pallas_call(
        kern,
        grid_spec=pltpu.PrefetchScalarGridSpec(
            num_scalar_prefetch=3,
            grid=(len(pairs),),
            in_specs=[
                pl.BlockSpec((QB, W), lambda s, qi, kj, fin: (qi[s], 0)),
                pl.BlockSpec((TK, W), lambda s, qi, kj, fin: (kj[s], 0)),
                pl.BlockSpec((TK, W), lambda s, qi, kj, fin: (kj[s], 0)),
                pl.BlockSpec((1, 1, QB, TK), lambda s, qi, kj, fin: (qi[s], kj[s], 0, 0)),
            ],
            out_specs=pl.BlockSpec((QB, W), lambda s, qi, kj, fin: (qi[s], 0)),
            scratch_shapes=[pltpu.VMEM((H, QB, LANES), F32), pltpu.VMEM((H, QB, LANES), F32),
                            pltpu.VMEM((QB, W), F32)],
        ),
        out_shape=jax.ShapeDtypeStruct((S, W), BF16),
        compiler_params=_cparams("arbitrary"),
        name="dsa_attn_prompt",
    )(qi, kj, fin, aq, ak, av, bias)


def _idx_sample_kernel(pt_ref, iq_ref, w_ref, *refs):
    page_refs, o_ref = refs[:-1], refs[-1]
    iq = iq_ref[0].astype(BF16)
    w = w_ref[0]
    rows = []
    for page_ref in page_refs:
        s = _dot_nt(iq, page_ref[0, 0].astype(BF16))
        rows.append(jnp.sum(jnp.maximum(s, 0.0) * w, axis=0, keepdims=True))
    o_ref[0, 0] = jnp.concatenate(rows, axis=1)


def _idx_sample(cache_idx_k, layer, page_table, iq, iw):
    DB, NP = page_table.shape
    P, Di = cache_idx_k.shape[2:]
    Hi = IDX_HEADS
    G = _pick(NP, (16, 8, 4, 2, 1))
    w = jnp.broadcast_to((iw * (IDX_HEADS ** -0.5 * IDX_DIM ** -0.5))[:, :, None], (DB, Hi, P))

    def page_spec(r):
        return pl.BlockSpec((1, 1, P, Di), lambda b, g, pt: (layer, pt[b, g * G + r], 0, 0))

    out = pl.pallas_call(
        _idx_sample_kernel,
        grid_spec=pltpu.PrefetchScalarGridSpec(
            num_scalar_prefetch=1,
            grid=(DB, NP // G),
            in_specs=[
                pl.BlockSpec((1, Hi, Di), lambda b, g, pt: (b, 0, 0)),
                pl.BlockSpec((1, Hi, P), lambda b, g, pt: (b, 0, 0)),
            ] + [page_spec(r) for r in range(G)],
            out_specs=pl.BlockSpec((1, 1, 1, G * P), lambda b, g, pt: (b, g, 0, 0)),
        ),
        out_shape=jax.ShapeDtypeStruct((DB, NP // G, 1, G * P), F32),
        compiler_params=_cparams("parallel", "arbitrary"),
        name="dsa_index_sample",
    )(page_table, iq.reshape(DB, Hi, Di), w, *([cache_idx_k] * G))
    return out.reshape(DB, NP * P)


def _select_sample_kernel(sc_ref, iq_ref, ikn_ref, iw_ref, idx_ref, keys_ref, cnt_ref, *, ksel, past_len,
                          col_bits):
    NC, R, CW = keys_ref.shape
    NP = NC - 1
    w = iw_ref[...] * (IDX_HEADS ** -0.5 * IDX_DIM ** -0.5)
    ikn = ikn_ref[...]
    new = jnp.zeros((R, 1), F32)
    for h in range(IDX_HEADS):
        s = jnp.sum(iq_ref[:, h * IDX_DIM:(h + 1) * IDX_DIM] * ikn, axis=-1, keepdims=True)
        new = new + jnp.maximum(s, 0.0) * w[:, h:h + 1]

    def col_of(c):
        return c * CW + lax.broadcasted_iota(I32, (R, CW), 1)

    def load(c, carry):
        keys_ref[c] = _sortable_key(sc_ref[c] + 0.0)
        return carry

    lax.fori_loop(0, NP, load, 0)
    keys_ref[NP] = jnp.where(col_of(NP) <= past_len, _sortable_key(jnp.broadcast_to(new, (R, CW))), INT_MIN)
    T, J = _topk_select(keys_ref, NC, ksel, col_bits)

    tri = (lax.broadcasted_iota(I32, (CW, CW), 0) <= lax.broadcasted_iota(I32, (CW, CW), 1)).astype(BF16)

    def prefix(c, before):
        key = keys_ref[c]
        col = col_of(c)
        sel = ((key > T) | ((key == T) & (col <= J))) & (col <= past_len)
        f = jnp.where(sel, 1.0, 0.0)
        cnt_ref[c] = _dot(f.astype(BF16), tri) + before
        return before + jnp.sum(f, axis=-1, keepdims=True)

    lax.fori_loop(0, NC, prefix, jnp.zeros((R, 1), F32))

    jcol = lax.broadcasted_iota(I32, (ksel, 1), 0).astype(F32)
    lane = lax.broadcasted_iota(I32, (ksel, LANES), 1)

    def one_row(r, out):
        def chunk(c, acc):
            return acc + jnp.where(cnt_ref[c, pl.ds(r, 1), :] <= jcol, 1.0, 0.0)
        acc = lax.fori_loop(0, NC, chunk, jnp.zeros((ksel, CW), F32))
        return jnp.where(lane == r, jnp.sum(acc, axis=-1, keepdims=True), out)

    idx_ref[...] = lax.fori_loop(0, R, one_row, jnp.zeros((ksel, LANES), F32)).astype(I32)


def _select_sample(scores, iq, ik_new, iw, ksel):
    DB = scores.shape[0]
    P = LANES
    NP = scores.shape[1] // P
    past_len = NP * P
    assert DB <= LANES
    sc = scores.reshape(DB, NP, P).transpose(1, 0, 2)
    kern = functools.partial(_select_sample_kernel, ksel=ksel, past_len=past_len,
                             col_bits=int(np.ceil(np.log2(past_len + P))))
    idx_t = pl.pallas_call(
        kern,
        out_shape=jax.ShapeDtypeStruct((ksel, LANES), I32),
        scratch_shapes=[pltpu.VMEM((NP + 1, DB, P), I32), pltpu.VMEM((NP + 1, DB, P), F32)],
        compiler_params=pltpu.CompilerParams(vmem_limit_bytes=VMEM_LIMIT_BYTES),
        name="dsa_select_sample",
    )(sc, iq, ik_new, iw)
    return idx_t[:, :DB].T


def _attn_sample_kernel(idx_ref, pt_ref, q_ref, kn_ref, vn_ref, *refs, G, ksel, past_len):
    k_refs, v_refs = refs[:G], refs[G:2 * G]
    o_ref, m_sc, l_sc, acc_sc = refs[2 * G:]
    b = pl.program_id(0)
    g = pl.program_id(1)
    scale = q_ref.shape[-1] ** -0.5

    @pl.when(g == 0)
    def _():
        m_sc[...] = jnp.full(m_sc.shape, NEG_BIAS, F32)
        l_sc[...] = jnp.zeros_like(l_sc)
        acc_sc[...] = jnp.zeros_like(acc_sc)

    q = q_ref[0]

    def slot_bias(r):
        pos = idx_ref[b * ksel + g * G + r]
        return jnp.full((1, 1, 1), jnp.where(pos >= past_len, NEG_BIAS, 0.0), F32)

    def update(kk, vv, bias3):
        s = jnp.sum(kk * q[None], axis=-1, keepdims=True) * scale + bias3
        m_prev = m_sc[...]
        m_new = jnp.maximum(m_prev, jnp.max(s, axis=0))
        alpha = jnp.exp(m_prev - m_new)
        pr = jnp.exp(s - m_new[None])
        l_sc[...] = alpha * l_sc[...] + jnp.sum(pr, axis=0)
        acc_sc[...] = alpha * acc_sc[...] + jnp.sum(pr * vv, axis=0)
        m_sc[...] = m_new

    kk = jnp.concatenate([r[0, 0] for r in k_refs], axis=0)
    vv = jnp.concatenate([r[0, 0] for r in v_refs], axis=0)
    update(kk, vv, jnp.concatenate([slot_bias(r) for r in range(G)], axis=0))

    @pl.when(g == pl.num_programs(1) - 1)
    def _():
        last = idx_ref[b * ksel + ksel - 1]
        bnew = jnp.full((1, 1, 1), jnp.where(last >= past_len, 0.0, NEG_BIAS), F32)
        update(kn_ref[...], vn_ref[...], bnew)
        o_ref[0] = acc_sc[...] / l_sc[...]


def _attn_sample(cache_k, cache_v, layer, page_table, aq, ak_new, av_new, idx):
    DB, NP = page_table.shape
    P, H, DH = cache_k.shape[2:]
    ksel = idx.shape[1]
    past_len = NP * P
    G = _pick(ksel, (32, 16, 8, 4, 2, 1))

    def row_spec(r):
        def index_map(b, g, idx_ref, pt):
            pos = jnp.minimum(idx_ref[b * ksel + g * G + r], past_len - 1)
            return (layer, pt[b, pos // P], pos % P, 0, 0)
        return pl.BlockSpec((1, 1, 1, H, DH), index_map)

    vec_spec = pl.BlockSpec((1, H, DH), lambda b, g, idx_ref, pt: (b, 0, 0))
    kern = functools.partial(_attn_sample_kernel, G=G, ksel=ksel, past_len=past_len)
    out = pl.pallas_call(
        kern,
        grid_spec=pltpu.PrefetchScalarGridSpec(
            num_scalar_prefetch=2,
            grid=(DB, ksel // G),
            in_specs=[vec_spec, vec_spec, vec_spec] + [row_spec(r) for r in range(G)] * 2,
            out_specs=vec_spec,
            scratch_shapes=[pltpu.VMEM((H, 1), F32), pltpu.VMEM((H, 1), F32), pltpu.VMEM((H, DH), F32)],
        ),
        out_shape=jax.ShapeDtypeStruct((DB, H, DH), F32),
        compiler_params=_cparams("parallel", "arbitrary"),
        name="dsa_attn_sample",
    )(idx.reshape(-1), page_table, aq.reshape(DB, H, DH), ak_new.reshape(DB, H, DH), av_new.reshape(DB, H, DH),
      *([cache_k] * G), *([cache_v] * G))
    return out.reshape(DB, H * DH)


def _rotary_tables(pos):
    d = RET_DK
    inv_freq = 1.0 / (ROPE_BASE ** jnp.linspace(0.0, 1.0, d // 2, dtype=F32))
    ang = pos.astype(F32)[:, None] * inv_freq[None, :]
    cos = jnp.repeat(jnp.cos(ang), 2, axis=1)
    sin = jnp.sin(ang)
    sin_signed = jnp.stack([-sin, sin], axis=-1).reshape(pos.shape[0], d)
    return cos, sin_signed


def _in_proj_columns(w_in):
    rqk, rv_w = RET_HEADS * RET_DK, RET_HEADS * RET_DV
    aw, iqw = ATT_HEADS * ATT_DH, IDX_HEADS * IDX_DIM
    d_model = w_in.shape[1]
    names = ("rq", "rk", "rv", "rg", "aq", "ak", "av", "iq", "ik", "iw", "ga", "gb")
    sizes = (rqk, rqk, rv_w, rv_w, aw, aw, aw, iqw, IDX_DIM, IDX_HEADS, d_model, d_model)
    offs = np.concatenate([[0], np.cumsum(sizes)]).tolist()
    assert offs[-1] == w_in.shape[2]
    return {n: (offs[t], sizes[t]) for t, n in enumerate(names)}


def _mix_inputs(h, w_in_t, layer, cols, qg, kg, cos, sin_signed, act_dt, q_scale, want_bf16_v):
    M = h.shape[0]
    tm = _pick(M, ROW_TILES)

    def proj(name, epi, extras, dts, tn_prefs=(512, 256, 128), n=None):
        col0 = cols[name][0]
        n = n or cols[name][1]
        return _mm([h], [(0, w_in_t, layer, col0, n, True)], extras, epi, dts, tm=tm, tn=_pick(n, tn_prefs),
                   name="proj_" + name)

    rot = [(cos, "rows"), (sin_signed, "rows")]
    (rq,) = proj("rq", functools.partial(_epi_rotary, 1.0), rot, [act_dt])
    (rk,) = proj("rk", functools.partial(_epi_rotary, RET_DK ** -0.5), rot, [act_dt])
    (rv,) = proj("rv", _epi_identity, [], [act_dt])
    (rg,) = proj("rg", _epi_identity, [], [F32])
    (aq,) = proj("aq", functools.partial(_epi_head_norm, 1), [(qg.reshape(1, -1) * q_scale, "full")], [act_dt])
    ak, ak16 = proj("ak", functools.partial(_epi_head_norm, 2), [(kg.reshape(1, -1), "full")], [F32, BF16])
    if want_bf16_v:
        av, av16 = proj("av", _epi_two_copies, [], [F32, BF16])
    else:
        (av,) = proj("av", _epi_identity, [], [F32])
        av16 = None
    (iq,) = proj("iq", _epi_identity, [], [act_dt])
    (ikw,) = proj("ik", _epi_identity, [], [F32], tn_prefs=(2 * LANES,), n=2 * LANES)
    (ga,) = proj("ga", _epi_identity, [], [F32])
    (gb,) = proj("gb", _epi_identity, [], [F32])
    ik = ikw[:, :IDX_DIM]
    iw = ikw[:, IDX_DIM:IDX_DIM + IDX_HEADS]
    return rq, rk, rv, rg, aq, ak, ak16, av, av16, iq, ik, iw, ga, gb


def _merge_ffn(x, yr, ya, ga, gb, layer, w_ret_o, w_att_o, w_out, g_ffn, w_gate, w_up, wd):
    M, D = x.shape
    F = w_gate.shape[2]
    tm = _pick(M, ROW_TILES)
    (m,) = _mm([yr, ya], [(0, w_ret_o, layer, 0, D, False), (1, w_att_o, layer, 0, D, False)],
               [(ga, "tile"), (gb, "tile")], _epi_merge, [BF16], tm=tm, tn=_pick(D, (256, 128)), name="merge")
    (x1,) = _mm([m], [(0, w_out, layer, 0, D, False)], [(x, "tile")], _epi_residual, [F32], tm=tm,
                tn=_pick(D, (512, 256, 128)), name="out_proj")
    hf = _rmsnorm(x1, g_ffn)
    (u,) = _mm([hf], [(0, w_gate, layer, 0, F, False), (0, w_up, layer, 0, F, False)], [], _epi_swiglu, [BF16],
               tm=tm, tn=_pick(F, (256, 128)), name="ffn_up")
    (y,) = _mm([u], [(0, wd, None, 0, D, False)], [(x1, "tile")], _epi_residual, [F32], tm=_pick(M, ROW_TILES[1:]),
               tn=_pick(D, (256, 128)), name="ffn_down", hold_b=False)
    return y


def kernel(x_prompt, x_sample, cache_k, cache_v, cache_idx_k, state_ret, page_table, norm_mix_g, w_in,
           q_norm_g, k_norm_g, w_ret_o, w_att_o, w_out, norm_ffn_g, w_ffn_gate, w_ffn_up, w_ffn_down):
    depth = w_in.shape[0]
    B, S, D = x_prompt.shape
    DB, T, _ = x_sample.shape
    assert B == 1 and T == 1
    page_size = cache_k.shape[2]
    past_len = page_table.shape[1] * page_size

    log_gamma = jnp.log1p(-jnp.exp2(-5.0 - jnp.arange(RET_HEADS, dtype=F32)))
    ret_tabs = _retention_tables(log_gamma)
    cos_p, sin_p = _rotary_tables(jnp.arange(S, dtype=I32))
    cos_s, sin_s = _rotary_tables(jnp.full((DB,), past_len, I32))
    ksel_p = min(TOPK_MAX, S // 4)
    ksel_s = min(TOPK_MAX, (past_len + T) // 4)
    cw = _pick(S, (512, 256, 128))
    qb = Q_BLOCK
    cols = _in_proj_columns(w_in)
    w_in_t = jnp.swapaxes(w_in, 1, 2)

    xp = x_prompt.reshape(S, D)
    xs = x_sample.reshape(DB, D)
    kp, vp, ikp, stp, ksm, vsm, iks, sts = [], [], [], [], [], [], [], []
    for layer in range(depth):
        wd = w_ffn_down[layer].astype(BF16)
        dense = (layer, w_ret_o, w_att_o, w_out, norm_ffn_g[layer], w_ffn_gate, w_ffn_up, wd)

        h = _rmsnorm(xp, norm_mix_g[layer])
        rq, rk, rv, rg, aq, ak, ak16, av, av16, iq, ik, iw, ga, gb = _mix_inputs(
            h, w_in_t, layer, cols, q_norm_g[layer], k_norm_g[layer], cos_p, sin_p, BF16,
            ATT_DH ** -0.5 * float(np.log2(np.e)), True)
        yr, st_p = _retention_prompt(rq, rk, rv, rg, ret_tabs)
        bias = _dsa_select_prompt(iq, iw, ik.astype(BF16), ksel_p, qb, cw)
        ya = _dsa_attn_prompt(aq, ak16, av16, bias, qb, cw)
        xp = _merge_ffn(xp, yr, ya, ga, gb, *dense)
        kp.append(ak.reshape(B, S, ATT_HEADS, ATT_DH))
        vp.append(av.reshape(B, S, ATT_HEADS, ATT_DH))
        ikp.append(ik.reshape(B, S, IDX_DIM))
        stp.append(st_p[None].astype(state_ret.dtype))

        h = _rmsnorm(xs, norm_mix_g[layer])
        rq, rk, rv, rg, aq, ak, _, av, _, iq, ik, iw, ga, gb = _mix_inputs(
            h, w_in_t, layer, cols, q_norm_g[layer], k_norm_g[layer], cos_s, sin_s, F32, 1.0, False)
        yr, st_s = _retention_sample(state_ret, layer, rq, rk, rv, rg, log_gamma)
        scores = _idx_sample(cache_idx_k, layer, page_table, iq, iw)
        sel_idx = _select_sample(scores, iq, ik, iw, ksel_s)
        ya = _attn_sample(cache_k, cache_v, layer, page_table, aq, ak, av, sel_idx)
        xs = _merge_ffn(xs, yr.astype(BF16), ya.astype(BF16), ga, gb, *dense)
        ksm.append(ak.reshape(DB, T, ATT_HEADS, ATT_DH))
        vsm.append(av.reshape(DB, T, ATT_HEADS, ATT_DH))
        iks.append(ik.reshape(DB, T, IDX_DIM))
        sts.append(st_s)
    return (xp.reshape(B, S, D), xs.reshape(DB, T, D), jnp.stack(kp), jnp.stack(vp), jnp.stack(ikp),
            jnp.stack(stp), jnp.stack(ksm), jnp.stack(vsm), jnp.stack(iks), jnp.stack(sts))
```

```python
import functools

import numpy as np
import jax
import jax.numpy as jnp
from jax import lax
from jax.experimental import pallas as pl
from jax.experimental.pallas import tpu as pltpu

F32 = jnp.float32
BF16 = jnp.bfloat16
I32 = jnp.int32

RET_HEADS = 8
RET_DK = 256
RET_DV = 512
RET_CHUNK = 128
ROPE_BASE = 10000.0
ATT_HEADS = 16
ATT_DH = 128
IDX_HEADS = 8
IDX_DIM = 128
TOPK_MAX = 256
Q_BLOCK = 128
EPS = 1e-6

LANES = 128
VMEM_LIMIT_BYTES = 56 * 1024 * 1024

ROW_TILES = (1024, 512, 256, 128, 64, 32, 16, 8)

NEG_BIAS = -1e30
INT_MIN = -(2 ** 31)


def _cparams(*sem):
    return pltpu.CompilerParams(dimension_semantics=sem, vmem_limit_bytes=VMEM_LIMIT_BYTES)


def _pick(n, prefs):
    for p in prefs:
        if p <= n and n % p == 0:
            return p
    return n


def _dot(a, b):
    return jnp.dot(a, b, preferred_element_type=F32)


def _dot_nt(a, b):
    return lax.dot_general(a, b, (((1,), (1,)), ((), ())), preferred_element_type=F32)


def _sigmoid(x):
    return 1.0 / (1.0 + jnp.exp(-x))


def _silu(x):
    return x * _sigmoid(x)


def _rmsnorm_kernel(x_ref, g_ref, o_ref):
    x = x_ref[...]
    ms = jnp.mean(x * x, axis=-1, keepdims=True)
    o_ref[...] = (x * lax.rsqrt(ms + EPS) * g_ref[...]).astype(o_ref.dtype)


def _rmsnorm(x, g):
    M, D = x.shape
    tm = _pick(M, (256, 128, 64, 32, 16, 8))
    return pl.pallas_call(
        _rmsnorm_kernel,
        grid=(M // tm,),
        in_specs=[pl.BlockSpec((tm, D), lambda i: (i, 0)), pl.BlockSpec((1, D), lambda i: (0, 0))],
        out_specs=pl.BlockSpec((tm, D), lambda i: (i, 0)),
        out_shape=jax.ShapeDtypeStruct((M, D), BF16),
        compiler_params=_cparams("parallel"),
        name="rmsnorm",
    )(x, g.reshape(1, D))


def _mm_kernel(*refs, n_a, b_src, b_nt, n_e, n_o, cast, epilogue, rider):
    n_b = len(b_src)
    take = lambda n, it=iter(refs): [next(it) for _ in range(n)]
    a_refs, b_refs, e_refs = take(n_a), take(n_b), take(n_e)
    if rider:
        n_e2, n_o2, epilogue2 = rider
        a2_refs, e2_refs = take(n_a), take(n_e2)
    o_refs = take(n_o)
    if rider:
        o2_refs = take(n_o2)
    if cast:
        w_refs = take(n_b)

    def compute(a_rs, e_rs, o_rs, epi):
        bs = w_refs if cast else b_refs
        accs = [(_dot_nt if nt else _dot)(a_rs[s][...], b[...]) for s, nt, b in zip(b_src, b_nt, bs)]
        for o, r in zip(o_rs, epi(accs, [e[...] for e in e_rs])):
            o[...] = r.astype(o.dtype)

    if cast or rider:
        @pl.when(pl.program_id(1) == 0)
        def _():
            if cast:
                for b, w in zip(b_refs, w_refs):
                    w[...] = b[...].reshape(w.shape).astype(w.dtype)
            if rider:
                compute(a2_refs, e2_refs, o2_refs, epilogue2)
    compute(a_refs, e_refs, o_refs, epilogue)


def _mm(a_list, b_list, extras, epilogue, out_dtypes, *, tm, tn, name, hold_b=True, rider=None):
    M = a_list[0].shape[0]
    N = b_list[0][4]
    cast = b_list[0][2] is not None
    assert all((b[2] is not None) == cast for b in b_list) and (hold_b or not (cast or rider))
    if hold_b:
        grid = (N // tn, M // tm)
        ij = lambda f: (lambda j, i: f(i, j))
    else:
        grid = (M // tm, N // tn)
        ij = lambda f: f
    in_specs, args, scratch = [], [], []
    for a in a_list:
        in_specs.append(pl.BlockSpec((tm, a.shape[1]), ij(lambda i, j: (i, 0))))
        args.append(a)
    for _, w, layer, col0, _, nt in b_list:
        if nt:
            K = w.shape[2]
            if col0 % tn == 0:
                spec = pl.BlockSpec((None, tn, K), ij(lambda i, j, l=layer, c=col0 // tn: (l, c + j, 0)))
            else:
                spec = pl.BlockSpec((pl.Element(1), pl.Element(tn), pl.Element(K)),
                                    ij(lambda i, j, l=layer, c=col0: (l, pl.multiple_of(c + j * tn, 8), 0)))
            scratch.append(pltpu.VMEM((tn, K), BF16))
        elif cast:
            assert col0 % tn == 0
            K = w.shape[1]
            spec = pl.BlockSpec((None, K, tn), ij(lambda i, j, l=layer, c=col0 // tn: (l, 0, c + j)))
            scratch.append(pltpu.VMEM((K, tn), BF16))
        else:
            assert col0 % tn == 0
            spec = pl.BlockSpec((w.shape[0], tn), ij(lambda i, j, c=col0 // tn: (0, c + j)))
        in_specs.append(spec)
        args.append(w)
    def add_extras(ex_list, rows, row_index):
        for arr, kind in ex_list:
            if kind == "tile":
                in_specs.append(pl.BlockSpec((rows, tn), ij(lambda i, j: (row_index(i), j))))
            elif kind == "rows":
                in_specs.append(pl.BlockSpec((rows, arr.shape[1]), ij(lambda i, j: (row_index(i), 0))))
            else:
                in_specs.append(pl.BlockSpec(arr.shape, ij(lambda i, j, nd=arr.ndim: (0,) * nd)))
            args.append(arr)

    add_extras(extras, tm, lambda i: i)
    out_specs = [pl.BlockSpec((tm, tn), ij(lambda i, j: (i, j))) for _ in out_dtypes]
    out_shape = [jax.ShapeDtypeStruct((M, N), od) for od in out_dtypes]
    if rider:
        a_list2, extras2, epilogue2, out_dtypes2 = rider
        M2 = a_list2[0].shape[0]
        for a in a_list2:
            in_specs.append(pl.BlockSpec((M2, a.shape[1]), ij(lambda i, j: (0, 0))))
            args.append(a)
        add_extras(extras2, M2, lambda i: 0)
        out_specs += [pl.BlockSpec((M2, tn), ij(lambda i, j: (0, j))) for _ in out_dtypes2]
        out_shape += [jax.ShapeDtypeStruct((M2, N), od) for od in out_dtypes2]
        rider = (len(extras2), len(out_dtypes2), epilogue2)
    kern = functools.partial(_mm_kernel, n_a=len(a_list), b_src=tuple(b[0] for b in b_list),
                             b_nt=tuple(b[5] for b in b_list), n_e=len(extras), n_o=len(out_dtypes), cast=cast,
                             epilogue=epilogue, rider=rider)
    outs = pl.pallas_call(
        kern,
        grid=grid,
        in_specs=in_specs,
        out_specs=out_specs,
        out_shape=out_shape,
        scratch_shapes=scratch,
        compiler_params=_cparams("parallel", "arbitrary"),
        name=name,
    )(*args)
    return outs


def _epi_identity(accs, extras):
    return [accs[0]]


def _epi_two_copies(accs, extras):
    return [accs[0], accs[0]]


def _epi_rotary(scale, accs, extras):
    acc = accs[0]
    cos, sin_signed = extras
    dk = cos.shape[1]
    lane = lax.broadcasted_iota(I32, (acc.shape[0], dk), 1)
    even = (lane % 2) == 0
    outs = []
    for h in range(acc.shape[1] // dk):
        x = acc[:, h * dk:(h + 1) * dk]
        swapped = jnp.where(even, pltpu.roll(x, dk - 1, 1), pltpu.roll(x, 1, 1))
        outs.append((x * cos + swapped * sin_signed) * scale)
    return [jnp.concatenate(outs, axis=1) if len(outs) > 1 else outs[0]]


def _epi_head_norm(n_out, accs, extras):
    acc = accs[0]
    g = extras[0]
    dh = g.shape[1]
    outs = []
    for h in range(acc.shape[1] // dh):
        x = acc[:, h * dh:(h + 1) * dh]
        ms = jnp.mean(x * x, axis=-1, keepdims=True)
        outs.append(x * lax.rsqrt(ms + EPS) * g)
    y = jnp.concatenate(outs, axis=1) if len(outs) > 1 else outs[0]
    return [y] * n_out


def _epi_merge(accs, extras):
    ga, gb = extras
    return [_sigmoid(ga) * accs[0] + _sigmoid(gb) * accs[1]]


def _epi_residual(accs, extras):
    return [extras[0] + accs[0]]


def _epi_swiglu(accs, extras):
    return [_silu(accs[0]) * accs[1]]


def _ret_prompt_kernel(q_ref, k_ref, v_ref, rg_ref, dec_ref, cd_ref, kd_ref, gc_ref, yr_ref, st_ref,
                       *, nsub, C):
    @pl.when(pl.program_id(1) == 0)
    def _():
        st_ref[...] = jnp.zeros_like(st_ref)

    dec = dec_ref[0]
    cd = cd_ref[0]
    kd = kd_ref[0]
    gc = gc_ref[0]
    for t in range(nsub):
        rows = slice(t * C, (t + 1) * C)
        q = q_ref[rows, :]
        k = k_ref[rows, :]
        v = v_ref[rows, :]
        st = st_ref[0]
        s = _dot_nt(q, k) * dec
        o = _dot(s.astype(BF16), v) + _dot(q, st.astype(BF16)) * cd
        kdt = (k.astype(F32) * kd).T.astype(BF16)
        st_ref[0] = gc * st + _dot(kdt, v)
        ms = jnp.mean(o * o, axis=-1, keepdims=True)
        yr_ref[rows, :] = (o * lax.rsqrt(ms + EPS) * _silu(rg_ref[rows, :])).astype(yr_ref.dtype)


def _retention_prompt(rq, rk, rv, rg, tabs):
    S = rq.shape[0]
    H, DK, DV, C = RET_HEADS, RET_DK, RET_DV, RET_CHUNK
    nsub = _pick(S // C, (4, 2, 1))
    T = nsub * C
    dec, cd, kd, gc = tabs
    kern = functools.partial(_ret_prompt_kernel, nsub=nsub, C=C)
    return pl.pallas_call(
        kern,
        grid=(H, S // T),
        in_specs=[
            pl.BlockSpec((T, DK), lambda h, c: (c, h)),
            pl.BlockSpec((T, DK), lambda h, c: (c, h)),
            pl.BlockSpec((T, DV), lambda h, c: (c, h)),
            pl.BlockSpec((T, DV), lambda h, c: (c, h)),
            pl.BlockSpec((1, C, C), lambda h, c: (h, 0, 0)),
            pl.BlockSpec((1, C, DV), lambda h, c: (h, 0, 0)),
            pl.BlockSpec((1, C, DK), lambda h, c: (h, 0, 0)),
            pl.BlockSpec((1, 1, DV), lambda h, c: (h, 0, 0)),
        ],
        out_specs=[
            pl.BlockSpec((T, DV), lambda h, c: (c, h)),
            pl.BlockSpec((1, DK, DV), lambda h, c: (h, 0, 0)),
        ],
        out_shape=[jax.ShapeDtypeStruct((S, H * DV), BF16), jax.ShapeDtypeStruct((H, DK, DV), F32)],
        compiler_params=_cparams("parallel", "arbitrary"),
        name="retention_prompt",
    )(rq, rk, rv, rg, dec, cd, kd, gc)


def _retention_tables(log_gamma):
    H, DK, DV, C = RET_HEADS, RET_DK, RET_DV, RET_CHUNK
    i = jnp.arange(C, dtype=F32)
    rel = i[:, None] - i[None, :]
    causal = rel >= 0
    dec = jnp.where(causal[None], jnp.exp(jnp.where(causal, rel, 0.0)[None] * log_gamma[:, None, None]), 0.0)
    cross = jnp.exp((i[:, None] + 1.0) * log_gamma[None, :])
    kdec = jnp.exp((C - 1.0 - i)[:, None] * log_gamma[None, :])
    gC = jnp.exp(C * log_gamma)
    cd = jnp.broadcast_to(cross.T[:, :, None], (H, C, DV))
    kd = jnp.broadcast_to(kdec.T[:, :, None], (H, C, DK))
    gc = jnp.broadcast_to(gC[:, None, None], (H, 1, DV))
    return dec, cd, kd, gc


def _ret_sample_kernel(st_ref, qt_ref, kt_ref, v_ref, rg_ref, g1_ref, o_ref, ns_ref, *, H):
    qt = qt_ref[0]
    kt = kt_ref[0]
    v = v_ref[0]
    rg = rg_ref[0]
    for h in range(H):
        st = st_ref[0, 0, h]
        qc = qt[:, h:h + 1]
        kc = kt[:, h:h + 1]
        vr = v[h:h + 1, :]
        g1 = g1_ref[h]
        cross = jnp.sum(st * qc, axis=0, keepdims=True)
        qk = jnp.sum(qc * kc, axis=0, keepdims=True)
        o = qk * vr + cross * g1
        ns_ref[0, 0, h] = g1 * st + kc * vr
        ms = jnp.mean(o * o, axis=-1, keepdims=True)
        o_ref[0, h:h + 1, :] = o * lax.rsqrt(ms + EPS) * _silu(rg[h:h + 1, :])


def _retention_sample(state_ret, layer, rq, rk, rv, rg, log_gamma):
    DB = rq.shape[0]
    H, DK, DV = RET_HEADS, RET_DK, RET_DV
    qt = rq.reshape(DB, H, DK).transpose(0, 2, 1)
    kt = rk.reshape(DB, H, DK).transpose(0, 2, 1)
    g1 = jnp.broadcast_to(jnp.exp(1.0 * log_gamma)[:, None, None], (H, 1, DV))
    kern = functools.partial(_ret_sample_kernel, H=H)
    yr, ns = pl.pallas_call(
        kern,
        grid=(DB,),
        in_specs=[
            pl.BlockSpec((1, 1, H, DK, DV), lambda b: (layer, b, 0, 0, 0)),
            pl.BlockSpec((1, DK, H), lambda b: (b, 0, 0)),
            pl.BlockSpec((1, DK, H), lambda b: (b, 0, 0)),
            pl.BlockSpec((1, H, DV), lambda b: (b, 0, 0)),
            pl.BlockSpec((1, H, DV), lambda b: (b, 0, 0)),
            pl.BlockSpec((H, 1, DV), lambda b: (0, 0, 0)),
        ],
        out_specs=[
            pl.BlockSpec((1, H, DV), lambda b: (b, 0, 0)),
            pl.BlockSpec((1, 1, H, DK, DV), lambda b: (0, b, 0, 0, 0)),
        ],
        out_shape=[jax.ShapeDtypeStruct((DB, H, DV), F32),
                   jax.ShapeDtypeStruct((1, DB, H, DK, DV), state_ret.dtype)],
        compiler_params=_cparams("parallel"),
        name="retention_sample",
    )(state_ret, qt, kt, rv.reshape(DB, H, DV), rg.reshape(DB, H, DV), g1)
    return yr.reshape(DB, H * DV), ns[0]


def _sortable_key(x):
    bits = pltpu.bitcast(x, I32)
    return bits ^ ((bits >> 31) & 0x7FFFFFFF)


def _topk_select(keys_ref, nvis, ksel, col_bits, rows_on_lanes=False):
    if rows_on_lanes:
        _, CW, R = keys_ref.shape
        vec, col_axis = (1, R), 0
    else:
        _, R, CW = keys_ref.shape
        vec, col_axis = (R, 1), 1
    ksel_f = float(ksel)

    def count(pred):
        def body(c, acc):
            f = jnp.where(pred(keys_ref[c], c), 1.0, 0.0)
            if rows_on_lanes:
                return acc + jnp.sum(f.reshape(CW // 8, 8, R), axis=0)
            for g in range(CW // LANES):
                acc = acc + f[:, g * LANES:(g + 1) * LANES]
            return acc
        acc = lax.fori_loop(0, nvis, body, jnp.zeros((8, R) if rows_on_lanes else (R, LANES), F32))
        return jnp.sum(acc, axis=col_axis, keepdims=True)

    def value_bit(bi, u):
        cand_u = u | lax.shift_left(jnp.int32(1), 31 - bi)
        cand = cand_u ^ INT_MIN
        cnt = count(lambda key, c: key >= cand)
        return jnp.where(cnt >= ksel_f, cand_u, u)

    u = lax.fori_loop(0, 32, value_bit, jnp.zeros(vec, I32))
    T = u ^ INT_MIN
    need = ksel_f - count(lambda key, c: key > T)
    ties = count(lambda key, c: key == T)

    def col_of(c):
        return c * CW + lax.broadcasted_iota(I32, keys_ref.shape[1:], col_axis)

    def col_search():
        def col_bit(bi, x):
            cand = x | lax.shift_left(jnp.int32(1), col_bits - 1 - bi)
            cnt = count(lambda key, c: (key == T) & (col_of(c) < cand))
            return jnp.where(cnt < need, cand, x)
        return lax.fori_loop(0, col_bits, col_bit, jnp.zeros(vec, I32))

    contested = jnp.max(jnp.where((ties > need) & (T != INT_MIN), 1.0, 0.0)) > 0.0
    J = lax.cond(contested, col_search, lambda: jnp.full(vec, 2 ** 30, I32))
    J = jnp.where(T == INT_MIN, -1, J)
    return T, J


def _dsa_select_kernel(iq_ref, iw_ref, ik_ref, bias_ref, keys_ref, *, ksel, col_bits):
    NC, QB, CW = keys_ref.shape
    i = pl.program_id(0)
    nvis = ((i + 1) * QB + CW - 1) // CW
    w = iw_ref[...] * (IDX_HEADS ** -0.5 * IDX_DIM ** -0.5)
    rowpos = i * QB + lax.broadcasted_iota(I32, (QB, 1), 0)

    def col_of(c):
        return c * CW + lax.broadcasted_iota(I32, (QB, CW), 1)

    def score_chunk(c, carry):
        kc = ik_ref[c]
        acc = jnp.zeros((QB, CW), F32)
        for h in range(IDX_HEADS):
            s = _dot_nt(iq_ref[:, h * IDX_DIM:(h + 1) * IDX_DIM], kc)
            acc = acc + jnp.maximum(s, 0.0) * w[:, h:h + 1]
        keys_ref[c] = jnp.where(col_of(c) <= rowpos, _sortable_key(acc), INT_MIN)
        return carry

    lax.fori_loop(0, nvis, score_chunk, 0)
    T, J = _topk_select(keys_ref, nvis, ksel, col_bits)

    def emit(c, carry):
        key = keys_ref[c]
        col = col_of(c)
        sel = ((key > T) | ((key == T) & (col <= J))) & (col <= rowpos)
        bias_ref[0, c] = jnp.where(sel, 0.0, NEG_BIAS).astype(bias_ref.dtype)
        return carry

    lax.fori_loop(0, nvis, emit, 0)

    def fill(c, carry):
        bias_ref[0, c] = jnp.full((QB, CW), NEG_BIAS, bias_ref.dtype)
        return carry

    lax.fori_loop(nvis, NC, fill, 0)


def _dsa_select_prompt(iq, iw, ik, ksel, QB, CW):
    S = iq.shape[0]
    NQ, NC = S // QB, S // CW
    kern = functools.partial(_dsa_select_kernel, ksel=ksel, col_bits=int(np.ceil(np.log2(S))))
    return pl.pallas_call(
        kern,
        grid=(NQ,),
        in_specs=[
            pl.BlockSpec((QB, iq.shape[1]), lambda i: (i, 0)),
            pl.BlockSpec((QB, iw.shape[1]), lambda i: (i, 0)),
            pl.BlockSpec((NC, CW, ik.shape[1]), lambda i: (0, 0, 0)),
        ],
        out_specs=pl.BlockSpec((1, NC, QB, CW), lambda i: (i, 0, 0, 0)),
        out_shape=jax.ShapeDtypeStruct((NQ, NC, QB, CW), F32),
        scratch_shapes=[pltpu.VMEM((NC, QB, CW), I32)],
        compiler_params=_cparams("parallel"),
        name="dsa_select_prompt",
    )(iq, iw, ik.reshape(NC, CW, ik.shape[1]))


def _dsa_attn_kernel(qi_ref, kj_ref, fin_ref, q_ref, k_ref, v_ref, b_ref, o_ref, m_sc, l_sc, acc_sc,
                     *, H, DH, TK, SUB):
    step = pl.program_id(0)

    @pl.when(kj_ref[step] == 0)
    def _():
        m_sc[...] = jnp.full(m_sc.shape, NEG_BIAS, F32)
        l_sc[...] = jnp.zeros_like(l_sc)
        acc_sc[...] = jnp.zeros_like(acc_sc)

    for t in range(TK // SUB):
        keys = slice(t * SUB, (t + 1) * SUB)
        for h in range(H):
            cols = slice(h * DH, (h + 1) * DH)
            s = _dot_nt(q_ref[:, cols], k_ref[keys, cols]) + b_ref[0, 0, :, keys]
            m_prev = m_sc[h]
            m_new = jnp.maximum(m_prev, jnp.max(s, axis=-1, keepdims=True))
            alpha = jnp.exp2(m_prev - m_new)
            p = jnp.exp2(s - jnp.concatenate([m_new] * (SUB // LANES), axis=1))
            l_sc[h] = alpha * l_sc[h] + jnp.sum(p, axis=-1, keepdims=True)
            acc_sc[:, cols] = alpha * acc_sc[:, cols] + _dot(p.astype(BF16), v_ref[keys, cols])
            m_sc[h] = m_new

    @pl.when(fin_ref[step] == 1)
    def _():
        for h in range(H):
            cols = slice(h * DH, (h + 1) * DH)
            o_ref[:, cols] = (acc_sc[:, cols] / l_sc[h]).astype(o_ref.dtype)


def _dsa_attn_prompt(aq, ak, av, bias, QB, TK):
    S, W = aq.shape
    H, DH = ATT_HEADS, ATT_DH
    assert DH == LANES
    pairs = [(i, j) for i in range(S // QB) for j in range(((i + 1) * QB - 1) // TK + 1)]
    qi = jnp.asarray([p[0] for p in pairs], I32)
    kj = jnp.asarray([p[1] for p in pairs], I32)
    fin = jnp.asarray([int(p[1] == ((p[0] + 1) * QB - 1) // TK) for p in pairs], I32)
    kern = functools.partial(_dsa_attn_kernel, H=H, DH=DH, TK=TK, SUB=_pick(TK, (256, 128)))
    return pl.pallas_call(
        kern,
        grid_spec=pltpu.PrefetchScalarGridSpec(
            num_scalar_prefetch=3,
            grid=(len(pairs),),
            in_specs=[
                pl.BlockSpec((QB, W), lambda s, qi, kj, fin: (qi[s], 0)),
                pl.BlockSpec((TK, W), lambda s, qi, kj, fin: (kj[s], 0)),
                pl.BlockSpec((TK, W), lambda s, qi, kj, fin: (kj[s], 0)),
                pl.BlockSpec((1, 1, QB, TK), lambda s, qi, kj, fin: (qi[s], kj[s], 0, 0)),
            ],
            out_specs=pl.BlockSpec((QB, W), lambda s, qi, kj, fin: (qi[s], 0)),
            scratch_shapes=[pltpu.VMEM((H, QB, LANES), F32), pltpu.VMEM((H, QB, LANES), F32),
                            pltpu.VMEM((QB, W), F32)],
        ),
        out_shape=jax.ShapeDtypeStruct((S, W), BF16),
        compiler_params=_cparams("arbitrary"),
        name="dsa_attn_prompt",
    )(qi, kj, fin, aq, ak, av, bias)


def _idx_sample_kernel(pt_ref, iq_ref, w_ref, *refs):
    page_refs, o_ref = refs[:-1], refs[-1]
    iq = iq_ref[0].astype(BF16)
    w = w_ref[0]
    rows = []
    for page_ref in page_refs:
        s = _dot_nt(iq, page_ref[0, 0].astype(BF16))
        rows.append(jnp.sum(jnp.maximum(s, 0.0) * w, axis=0, keepdims=True))
    o_ref[0, 0] = jnp.concatenate(rows, axis=1)


def _idx_sample(cache_idx_k, layer, page_table, iq, iw):
    DB, NP = page_table.shape
    P, Di = cache_idx_k.shape[2:]
    Hi = IDX_HEADS
    G = _pick(NP, (16, 8, 4, 2, 1))
    w = jnp.broadcast_to((iw * (IDX_HEADS ** -0.5 * IDX_DIM ** -0.5))[:, :, None], (DB, Hi, P))

    def page_spec(r):
        return pl.BlockSpec((1, 1, P, Di), lambda b, g, pt: (layer, pt[b, g * G + r], 0, 0))

    out = pl.pallas_call(
        _idx_sample_kernel,
        grid_spec=pltpu.PrefetchScalarGridSpec(
            num_scalar_prefetch=1,
            grid=(DB, NP // G),
            in_specs=[
                pl.BlockSpec((1, Hi, Di), lambda b, g, pt: (b, 0, 0)),
                pl.BlockSpec((1, Hi, P), lambda b, g, pt: (b, 0, 0)),
            ] + [page_spec(r) for r in range(G)],
            out_specs=pl.BlockSpec((1, 1, 1, G * P), lambda b, g, pt: (b, g, 0, 0)),
        ),
        out_shape=jax.ShapeDtypeStruct((DB, NP // G, 1, G * P), F32),
        compiler_params=_cparams("parallel", "arbitrary"),
        name="dsa_index_sample",
    )(page_table, iq.reshape(DB, Hi, Di), w, *([cache_idx_k] * G))
    return out.reshape(DB, NP * P)


def _select_sample_kernel(sc_ref, iq_ref, ikn_ref, iw_ref, idx_ref, keys_ref, cnt_ref, *, ksel, past_len,
                          col_bits):
    NC, R, CW = keys_ref.shape
    NP = NC - 1
    w = iw_ref[...] * (IDX_HEADS ** -0.5 * IDX_DIM ** -0.5)
    ikn = ikn_ref[...]
    new = jnp.zeros((R, 1), F32)
    for h in range(IDX_HEADS):
        s = jnp.sum(iq_ref[:, h * IDX_DIM:(h + 1) * IDX_DIM] * ikn, axis=-1, keepdims=True)
        new = new + jnp.maximum(s, 0.0) * w[:, h:h + 1]

    def col_of(c):
        return c * CW + lax.broadcasted_iota(I32, (R, CW), 1)

    def load(c, carry):
        keys_ref[c] = _sortable_key(sc_ref[c] + 0.0)
        return carry

    lax.fori_loop(0, NP, load, 0)
    keys_ref[NP] = jnp.where(col_of(NP) <= past_len, _sortable_key(jnp.broadcast_to(new, (R, CW))), INT_MIN)
    T, J = _topk_select(keys_ref, NC, ksel, col_bits)

    tri = (lax.broadcasted_iota(I32, (CW, CW), 0) <= lax.broadcasted_iota(I32, (CW, CW), 1)).astype(BF16)

    def prefix(c, before):
        key = keys_ref[c]
        col = col_of(c)
        sel = ((key > T) | ((key == T) & (col <= J))) & (col <= past_len)
        f = jnp.where(sel, 1.0, 0.0)
        cnt_ref[c] = _dot(f.astype(BF16), tri) + before
        return before + jnp.sum(f, axis=-1, keepdims=True)

    lax.fori_loop(0, NC, prefix, jnp.zeros((R, 1), F32))

    jcol = lax.broadcasted_iota(I32, (ksel, 1), 0).astype(F32)
    lane = lax.broadcasted_iota(I32, (ksel, LANES), 1)

    def one_row(r, out):
        def chunk(c, acc):
            return acc + jnp.where(cnt_ref[c, pl.ds(r, 1), :] <= jcol, 1.0, 0.0)
        acc = lax.fori_loop(0, NC, chunk, jnp.zeros((ksel, CW), F32))
        return jnp.where(lane == r, jnp.sum(acc, axis=-1, keepdims=True), out)

    idx_ref[...] = lax.fori_loop(0, R, one_row, jnp.zeros((ksel, LANES), F32)).astype(I32)


def _select_sample(scores, iq, ik_new, iw, ksel):
    DB = scores.shape[0]
    P = LANES
    NP = scores.shape[1] // P
    past_len = NP * P
    assert DB <= LANES
    sc = scores.reshape(DB, NP, P).transpose(1, 0, 2)
    kern = functools.partial(_select_sample_kernel, ksel=ksel, past_len=past_len,
                             col_bits=int(np.ceil(np.log2(past_len + P))))
    idx_t = pl.pallas_call(
        kern,
        out_shape=jax.ShapeDtypeStruct((ksel, LANES), I32),
        scratch_shapes=[pltpu.VMEM((NP + 1, DB, P), I32), pltpu.VMEM((NP + 1, DB, P), F32)],
        compiler_params=pltpu.CompilerParams(vmem_limit_bytes=VMEM_LIMIT_BYTES),
        name="dsa_select_sample",
    )(sc, iq, ik_new, iw)
    return idx_t[:, :DB].T


def _attn_sample_kernel(idx_ref, pt_ref, q_ref, kn_ref, vn_ref, ck_hbm, cv_hbm, o_ref, kbuf, vbuf, sem,
                        m_sc, l_sc, acc_sc, *, layer, G, ksel, past_len, page):
    b = pl.program_id(0)
    slot = b % 2
    scale = q_ref.shape[-1] ** -0.5

    def row_copies(seq, r, sl):
        pos = jnp.minimum(idx_ref[seq * ksel + r], past_len - 1)
        src = (layer, pt_ref[seq, pos // page], pos % page)
        return (pltpu.make_async_copy(ck_hbm.at[src], kbuf.at[sl, r], sem.at[sl, 0]),
                pltpu.make_async_copy(cv_hbm.at[src], vbuf.at[sl, r], sem.at[sl, 1]))

    def start_rows(seq, sl):
        def body(r, carry):
            for cp in row_copies(seq, r, sl):
                cp.start()
            return carry
        lax.fori_loop(0, ksel, body, 0)

    @pl.when(b == 0)
    def _():
        start_rows(0, 0)

    @pl.when(b + 1 < pl.num_programs(0))
    def _():
        start_rows(b + 1, 1 - slot)

    def wait_row(r, carry):
        for cp in row_copies(b, r, slot):
            cp.wait()
        return carry

    lax.fori_loop(0, ksel, wait_row, 0)

    m_sc[...] = jnp.full(m_sc.shape, NEG_BIAS, F32)
    l_sc[...] = jnp.zeros_like(l_sc)
    acc_sc[...] = jnp.zeros_like(acc_sc)
    q = q_ref[0]

    def slot_bias(r):
        pos = idx_ref[b * ksel + r]
        return jnp.full((1, 1, 1), jnp.where(pos >= past_len, NEG_BIAS, 0.0), F32)

    def update(kk, vv, bias3):
        s = jnp.sum(kk * q[None], axis=-1, keepdims=True) * scale + bias3
        m_prev = m_sc[...]
        m_new = jnp.maximum(m_prev, jnp.max(s, axis=0))
        alpha = jnp.exp(m_prev - m_new)
        pr = jnp.exp(s - m_new[None])
        l_sc[...] = alpha * l_sc[...] + jnp.sum(pr, axis=0)
        acc_sc[...] = alpha * acc_sc[...] + jnp.sum(pr * vv, axis=0)
        m_sc[...] = m_new

    for c in range(ksel // G):
        rows = pl.ds(c * G, G)
        update(kbuf[slot, rows], vbuf[slot, rows],
               jnp.concatenate([slot_bias(c * G + r) for r in range(G)], axis=0))

    last = idx_ref[b * ksel + ksel - 1]
    update(kn_ref[...], vn_ref[...], jnp.full((1, 1, 1), jnp.where(last >= past_len, 0.0, NEG_BIAS), F32))
    o_ref[0] = acc_sc[...] / l_sc[...]


def _attn_sample(cache_k, cache_v, layer, page_table, aq, ak_new, av_new, idx):
    DB, NP = page_table.shape
    P, H, DH = cache_k.shape[2:]
    ksel = idx.shape[1]
    vec_spec = pl.BlockSpec((1, H, DH), lambda b, idx_ref, pt: (b, 0, 0))
    kern = functools.partial(_attn_sample_kernel, layer=layer, G=_pick(ksel, (32, 16, 8, 4, 2, 1)), ksel=ksel,
                             past_len=NP * P, page=P)
    out = pl.pallas_call(
        kern,
        grid_spec=pltpu.PrefetchScalarGridSpec(
            num_scalar_prefetch=2,
            grid=(DB,),
            in_specs=[vec_spec, vec_spec, vec_spec, pl.BlockSpec(memory_space=pl.ANY),
                      pl.BlockSpec(memory_space=pl.ANY)],
            out_specs=vec_spec,
            scratch_shapes=[pltpu.VMEM((2, ksel, H, DH), F32), pltpu.VMEM((2, ksel, H, DH), F32),
                            pltpu.SemaphoreType.DMA((2, 2)),
                            pltpu.VMEM((H, 1), F32), pltpu.VMEM((H, 1), F32), pltpu.VMEM((H, DH), F32)],
        ),
        out_shape=jax.ShapeDtypeStruct((DB, H, DH), F32),
        compiler_params=_cparams("arbitrary"),
        name="dsa_attn_sample",
    )(idx.reshape(-1), page_table, aq.reshape(DB, H, DH), ak_new.reshape(DB, H, DH), av_new.reshape(DB, H, DH),
      cache_k, cache_v)
    return out.reshape(DB, H * DH)


def _rotary_tables(pos):
    d = RET_DK
    inv_freq = 1.0 / (ROPE_BASE ** jnp.linspace(0.0, 1.0, d // 2, dtype=F32))
    ang = pos.astype(F32)[:, None] * inv_freq[None, :]
    cos = jnp.repeat(jnp.cos(ang), 2, axis=1)
    sin = jnp.sin(ang)
    sin_signed = jnp.stack([-sin, sin], axis=-1).reshape(pos.shape[0], d)
    return cos, sin_signed


def _in_proj_columns(w_in):
    rqk, rv_w = RET_HEADS * RET_DK, RET_HEADS * RET_DV
    aw, iqw = ATT_HEADS * ATT_DH, IDX_HEADS * IDX_DIM
    d_model = w_in.shape[1]
    names = ("rq", "rk", "rv", "rg", "aq", "ak", "av", "iq", "ik", "iw", "ga", "gb")
    sizes = (rqk, rqk, rv_w, rv_w, aw, aw, aw, iqw, IDX_DIM, IDX_HEADS, d_model, d_model)
    offs = np.concatenate([[0], np.cumsum(sizes)]).tolist()
    assert offs[-1] == w_in.shape[2]
    return {n: (offs[t], sizes[t]) for t, n in enumerate(names)}


def _mix_inputs(hp, hs, w_in_t, layer, cols, qg, kg, rot_p, rot_s, q_scale):
    tm = _pick(hp.shape[0], ROW_TILES)
    P, Sm = {}, {}

    def proj(name, epi_p, ex_p, out_p, epi_s, ex_s, out_s, tn_prefs=(512, 256, 128), n=None):
        col0 = cols[name][0]
        n = n or cols[name][1]
        outs = _mm([hp], [(0, w_in_t, layer, col0, n, True)], ex_p, epi_p, [d for _, d in out_p], tm=tm,
                   tn=_pick(n, tn_prefs), name="proj_" + name, rider=([hs], ex_s, epi_s, [d for _, d in out_s]))
        for (key, _), o in zip(out_p, outs):
            P[key] = o
        for (key, _), o in zip(out_s, outs[len(out_p):]):
            Sm[key] = o

    def same(name, dt_p, **kw):
        proj(name, _epi_identity, [], [(name, dt_p)], _epi_identity, [], [(name, F32)], **kw)

    rot_p = [(t, "rows") for t in rot_p]
    rot_s = [(t, "rows") for t in rot_s]
    for name, scale in (("rq", 1.0), ("rk", RET_DK ** -0.5)):
        epi = functools.partial(_epi_rotary, scale)
        proj(name, epi, rot_p, [(name, BF16)], epi, rot_s, [(name, F32)])
    same("rv", BF16)
    same("rg", F32)
    norm1, norm2 = functools.partial(_epi_head_norm, 1), functools.partial(_epi_head_norm, 2)
    proj("aq", norm1, [(qg.reshape(1, -1) * q_scale, "full")], [("aq", BF16)],
         norm1, [(qg.reshape(1, -1), "full")], [("aq", F32)])
    kgain = [(kg.reshape(1, -1), "full")]
    proj("ak", norm2, kgain, [("ak", F32), ("ak16", BF16)], norm1, kgain, [("ak", F32)])
    proj("av", _epi_two_copies, [], [("av", F32), ("av16", BF16)], _epi_identity, [], [("av", F32)])
    same("iq", BF16)
    same("ik", F32, tn_prefs=(2 * LANES,), n=2 * LANES)
    same("ga", F32)
    same("gb", F32)
    for d in (P, Sm):
        ikw = d["ik"]
        d["ik"] = ikw[:, :IDX_DIM]
        d["iw"] = ikw[:, IDX_DIM:IDX_DIM + IDX_HEADS]
    return P, Sm


def _merge_ffn(xp, xs, mp, ms, layer, w_ret_o, w_att_o, w_out, g_ffn, w_gate, w_up, wd):
    M, D = xp.shape
    F = w_gate.shape[2]
    tm = _pick(M, ROW_TILES)
    gates = lambda t: [(t[2], "tile"), (t[3], "tile")]
    m_p, m_s = _mm(list(mp[:2]), [(0, w_ret_o, layer, 0, D, False), (1, w_att_o, layer, 0, D, False)], gates(mp),
                   _epi_merge, [BF16], tm=_pick(M, ROW_TILES[1:]), tn=_pick(D, (512, 256, 128)), name="merge",
                   rider=(list(ms[:2]), gates(ms), _epi_merge, [BF16]))
    x1p, x1s = _mm([m_p], [(0, w_out, layer, 0, D, False)], [(xp, "tile")], _epi_residual, [F32], tm=tm,
                   tn=_pick(D, (512, 256, 128)), name="out_proj",
                   rider=([m_s], [(xs, "tile")], _epi_residual, [F32]))
    hfp, hfs = _rmsnorm(x1p, g_ffn), _rmsnorm(x1s, g_ffn)
    up, us = _mm([hfp], [(0, w_gate, layer, 0, F, False), (0, w_up, layer, 0, F, False)], [], _epi_swiglu, [BF16],
                 tm=tm, tn=_pick(F, (256, 128)), name="ffn_up", rider=([hfs], [], _epi_swiglu, [BF16]))
    down = lambda u, x1: _mm([u], [(0, wd, None, 0, D, False)], [(x1, "tile")], _epi_residual, [F32],
                             tm=_pick(u.shape[0], ROW_TILES[1:]), tn=_pick(D, (256, 128)), name="ffn_down",
                             hold_b=False)[0]
    return down(up, x1p), down(us, x1s)


def kernel(x_prompt, x_sample, cache_k, cache_v, cache_idx_k, state_ret, page_table, norm_mix_g, w_in,
           q_norm_g, k_norm_g, w_ret_o, w_att_o, w_out, norm_ffn_g, w_ffn_gate, w_ffn_up, w_ffn_down):
    depth = w_in.shape[0]
    B, S, D = x_prompt.shape
    DB, T, _ = x_sample.shape
    assert B == 1 and T == 1
    page_size = cache_k.shape[2]
    past_len = page_table.shape[1] * page_size

    log_gamma = jnp.log1p(-jnp.exp2(-5.0 - jnp.arange(RET_HEADS, dtype=F32)))
    ret_tabs = _retention_tables(log_gamma)
    cos_p, sin_p = _rotary_tables(jnp.arange(S, dtype=I32))
    cos_s, sin_s = _rotary_tables(jnp.full((DB,), past_len, I32))
    ksel_p = min(TOPK_MAX, S // 4)
    ksel_s = min(TOPK_MAX, (past_len + T) // 4)
    cw = _pick(S, (512, 256, 128))
    qb = Q_BLOCK
    cols = _in_proj_columns(w_in)
    w_in_t = jnp.swapaxes(w_in, 1, 2)

    xp = x_prompt.reshape(S, D)
    xs = x_sample.reshape(DB, D)
    kp, vp, ikp, stp, ksm, vsm, iks, sts = [], [], [], [], [], [], [], []
    for layer in range(depth):
        wd = w_ffn_down[layer].astype(BF16)
        dense = (layer, w_ret_o, w_att_o, w_out, norm_ffn_g[layer], w_ffn_gate, w_ffn_up, wd)

        p, s = _mix_inputs(_rmsnorm(xp, norm_mix_g[layer]), _rmsnorm(xs, norm_mix_g[layer]), w_in_t, layer, cols,
                           q_norm_g[layer], k_norm_g[layer], (cos_p, sin_p), (cos_s, sin_s),
                           ATT_DH ** -0.5 * float(np.log2(np.e)))

        yr_p, st_p = _retention_prompt(p["rq"], p["rk"], p["rv"], p["rg"], ret_tabs)
        bias = _dsa_select_prompt(p["iq"], p["iw"], p["ik"].astype(BF16), ksel_p, qb, cw)
        ya_p = _dsa_attn_prompt(p["aq"], p["ak16"], p["av16"], bias, qb, cw)

        yr_s, st_s = _retention_sample(state_ret, layer, s["rq"], s["rk"], s["rv"], s["rg"], log_gamma)
        scores = _idx_sample(cache_idx_k, layer, page_table, s["iq"], s["iw"])
        sel_idx = _select_sample(scores, s["iq"], s["ik"], s["iw"], ksel_s)
        ya_s = _attn_sample(cache_k, cache_v, layer, page_table, s["aq"], s["ak"], s["av"], sel_idx)

        xp, xs = _merge_ffn(xp, xs, (yr_p, ya_p, p["ga"], p["gb"]),
                            (yr_s.astype(BF16), ya_s.astype(BF16), s["ga"], s["gb"]), *dense)
        kp.append(p["ak"].reshape(B, S, ATT_HEADS, ATT_DH))
        vp.append(p["av"].reshape(B, S, ATT_HEADS, ATT_DH))
        ikp.append(p["ik"].reshape(B, S, IDX_DIM))
        stp.append(st_p[None].astype(state_ret.dtype))
        ksm.append(s["ak"].reshape(DB, T, ATT_HEADS, ATT_DH))
        vsm.append(s["av"].reshape(DB, T, ATT_HEADS, ATT_DH))
        iks.append(s["ik"].reshape(DB, T, IDX_DIM))
        sts.append(st_s)
    return (xp.reshape(B, S, D), xs.reshape(DB, T, D), jnp.stack(kp), jnp.stack(vp), jnp.stack(ikp),
            jnp.stack(stp), jnp.stack(ksm), jnp.stack(vsm), jnp.stack(iks), jnp.stack(sts))
```

```python
import functools

import numpy as np
import jax
import jax.numpy as jnp
from jax import lax
from jax.experimental import pallas as pl
from jax.experimental.pallas import tpu as pltpu

F32 = jnp.float32
BF16 = jnp.bfloat16
I32 = jnp.int32

RET_HEADS = 8
RET_DK = 256
RET_DV = 512
RET_CHUNK = 128
ROPE_BASE = 10000.0
ATT_HEADS = 16
ATT_DH = 128
IDX_HEADS = 8
IDX_DIM = 128
TOPK_MAX = 256
Q_BLOCK = 128
EPS = 1e-6

LANES = 128
VMEM_LIMIT_BYTES = 56 * 1024 * 1024

ROW_TILES = (1024, 512, 256, 128, 64, 32, 16, 8)

NEG_BIAS = -1e30
INT_MIN = -(2 ** 31)


def _cparams(*sem):
    return pltpu.CompilerParams(dimension_semantics=sem, vmem_limit_bytes=VMEM_LIMIT_BYTES)


def _pick(n, prefs):
    for p in prefs:
        if p <= n and n % p == 0:
            return p
    return n


def _dot(a, b):
    return jnp.dot(a, b, preferred_element_type=F32)


def _dot_nt(a, b):
    return lax.dot_general(a, b, (((1,), (1,)), ((), ())), preferred_element_type=F32)


def _sigmoid(x):
    return 1.0 / (1.0 + jnp.exp(-x))


def _silu(x):
    return x * _sigmoid(x)


def _rmsnorm_kernel(x_ref, g_ref, o_ref):
    x = x_ref[...]
    ms = jnp.mean(x * x, axis=-1, keepdims=True)
    o_ref[...] = (x * lax.rsqrt(ms + EPS) * g_ref[...]).astype(o_ref.dtype)


def _rmsnorm(x, g):
    M, D = x.shape
    tm = _pick(M, (256, 128, 64, 32, 16, 8))
    return pl.pallas_call(
        _rmsnorm_kernel,
        grid=(M // tm,),
        in_specs=[pl.BlockSpec((tm, D), lambda i: (i, 0)), pl.BlockSpec((1, D), lambda i: (0, 0))],
        out_specs=pl.BlockSpec((tm, D), lambda i: (i, 0)),
        out_shape=jax.ShapeDtypeStruct((M, D), BF16),
        compiler_params=_cparams("parallel"),
        name="rmsnorm",
    )(x, g.reshape(1, D))


def _mm_kernel(*refs, n_a, b_src, b_nt, n_e, n_o, cast, epilogue, rider):
    n_b = len(b_src)
    take = lambda n, it=iter(refs): [next(it) for _ in range(n)]
    a_refs, b_refs, e_refs = take(n_a), take(n_b), take(n_e)
    if rider:
        n_e2, n_o2, epilogue2 = rider
        a2_refs, e2_refs = take(n_a), take(n_e2)
    o_refs = take(n_o)
    if rider:
        o2_refs = take(n_o2)
    if cast:
        w_refs = take(n_b)

    def compute(a_rs, e_rs, o_rs, epi):
        bs = w_refs if cast else b_refs
        accs = [(_dot_nt if nt else _dot)(a_rs[s][...], b[...]) for s, nt, b in zip(b_src, b_nt, bs)]
        for o, r in zip(o_rs, epi(accs, [e[...] for e in e_rs])):
            o[...] = r.astype(o.dtype)

    if cast or rider:
        @pl.when(pl.program_id(1) == 0)
        def _():
            if cast:
                for b, w in zip(b_refs, w_refs):
                    w[...] = b[...].reshape(w.shape).astype(w.dtype)
            if rider:
                compute(a2_refs, e2_refs, o2_refs, epilogue2)
    compute(a_refs, e_refs, o_refs, epilogue)


def _mm(a_list, b_list, extras, epilogue, out_dtypes, *, tm, tn, name, hold_b=True, rider=None):
    M = a_list[0].shape[0]
    N = b_list[0][4]
    cast = b_list[0][2] is not None
    assert all((b[2] is not None) == cast for b in b_list) and (hold_b or not (cast or rider))
    if hold_b:
        grid = (N // tn, M // tm)
        ij = lambda f: (lambda j, i: f(i, j))
    else:
        grid = (M // tm, N // tn)
        ij = lambda f: f
    in_specs, args, scratch = [], [], []
    for a in a_list:
        in_specs.append(pl.BlockSpec((tm, a.shape[1]), ij(lambda i, j: (i, 0))))
        args.append(a)
    for _, w, layer, col0, _, nt in b_list:
        if nt:
            K = w.shape[2]
            if col0 % tn == 0:
                spec = pl.BlockSpec((None, tn, K), ij(lambda i, j, l=layer, c=col0 // tn: (l, c + j, 0)))
            else:
                spec = pl.BlockSpec((pl.Element(1), pl.Element(tn), pl.Element(K)),
                                    ij(lambda i, j, l=layer, c=col0: (l, pl.multiple_of(c + j * tn, 8), 0)))
            scratch.append(pltpu.VMEM((tn, K), BF16))
        elif cast:
            assert col0 % tn == 0
            K = w.shape[1]
            spec = pl.BlockSpec((None, K, tn), ij(lambda i, j, l=layer, c=col0 // tn: (l, 0, c + j)))
            scratch.append(pltpu.VMEM((K, tn), BF16))
        else:
            assert col0 % tn == 0
            spec = pl.BlockSpec((w.shape[0], tn), ij(lambda i, j, c=col0 // tn: (0, c + j)))
        in_specs.append(spec)
        args.append(w)
    def add_extras(ex_list, rows, row_index):
        for arr, kind in ex_list:
            if kind == "tile":
                in_specs.append(pl.BlockSpec((rows, tn), ij(lambda i, j: (row_index(i), j))))
            elif kind == "rows":
                in_specs.append(pl.BlockSpec((rows, arr.shape[1]), ij(lambda i, j: (row_index(i), 0))))
            else:
                in_specs.append(pl.BlockSpec(arr.shape, ij(lambda i, j, nd=arr.ndim: (0,) * nd)))
            args.append(arr)

    add_extras(extras, tm, lambda i: i)
    out_specs = [pl.BlockSpec((tm, tn), ij(lambda i, j: (i, j))) for _ in out_dtypes]
    out_shape = [jax.ShapeDtypeStruct((M, N), od) for od in out_dtypes]
    if rider:
        a_list2, extras2, epilogue2, out_dtypes2 = rider
        M2 = a_list2[0].shape[0]
        for a in a_list2:
            in_specs.append(pl.BlockSpec((M2, a.shape[1]), ij(lambda i, j: (0, 0))))
            args.append(a)
        add_extras(extras2, M2, lambda i: 0)
        out_specs += [pl.BlockSpec((M2, tn), ij(lambda i, j: (0, j))) for _ in out_dtypes2]
        out_shape += [jax.ShapeDtypeStruct((M2, N), od) for od in out_dtypes2]
        rider = (len(extras2), len(out_dtypes2), epilogue2)
    kern = functools.partial(_mm_kernel, n_a=len(a_list), b_src=tuple(b[0] for b in b_list),
                             b_nt=tuple(b[5] for b in b_list), n_e=len(extras), n_o=len(out_dtypes), cast=cast,
                             epilogue=epilogue, rider=rider)
    outs = pl.pallas_call(
        kern,
        grid=grid,
        in_specs=in_specs,
        out_specs=out_specs,
        out_shape=out_shape,
        scratch_shapes=scratch,
        compiler_params=_cparams("parallel", "arbitrary"),
        name=name,
    )(*args)
    return outs


def _epi_identity(accs, extras):
    return [accs[0]]


def _epi_two_copies(accs, extras):
    return [accs[0], accs[0]]


def _epi_rotary(scale, accs, extras):
    acc = accs[0]
    cos, sin_signed = extras
    dk = cos.shape[1]
    lane = lax.broadcasted_iota(I32, (acc.shape[0], dk), 1)
    even = (lane % 2) == 0
    outs = []
    for h in range(acc.shape[1] // dk):
        x = acc[:, h * dk:(h + 1) * dk]
        swapped = jnp.where(even, pltpu.roll(x, dk - 1, 1), pltpu.roll(x, 1, 1))
        outs.append((x * cos + swapped * sin_signed) * scale)
    return [jnp.concatenate(outs, axis=1) if len(outs) > 1 else outs[0]]


def _epi_head_norm(n_out, accs, extras):
    acc = accs[0]
    g = extras[0]
    dh = g.shape[1]
    outs = []
    for h in range(acc.shape[1] // dh):
        x = acc[:, h * dh:(h + 1) * dh]
        ms = jnp.mean(x * x, axis=-1, keepdims=True)
        outs.append(x * lax.rsqrt(ms + EPS) * g)
    y = jnp.concatenate(outs, axis=1) if len(outs) > 1 else outs[0]
    return [y] * n_out


def _epi_merge(accs, extras):
    ga, gb = extras
    return [_sigmoid(ga) * accs[0] + _sigmoid(gb) * accs[1]]


def _epi_residual(accs, extras):
    return [extras[0] + accs[0]]


def _epi_swiglu(accs, extras):
    return [_silu(accs[0]) * accs[1]]


def _ret_prompt_kernel(q_ref, k_ref, v_ref, rg_ref, dec_ref, cd_ref, kd_ref, gc_ref, yr_ref, st_ref,
                       *, nsub, C):
    @pl.when(pl.program_id(1) == 0)
    def _():
        st_ref[...] = jnp.zeros_like(st_ref)

    dec = dec_ref[0]
    cd = cd_ref[0]
    kd = kd_ref[0]
    gc = gc_ref[0]
    for t in range(nsub):
        rows = slice(t * C, (t + 1) * C)
        q = q_ref[rows, :]
        k = k_ref[rows, :]
        v = v_ref[rows, :]
        st = st_ref[0]
        s = _dot_nt(q, k) * dec
        o = _dot(s.astype(BF16), v) + _dot(q, st.astype(BF16)) * cd
        kdt = (k.astype(F32) * kd).T.astype(BF16)
        st_ref[0] = gc * st + _dot(kdt, v)
        ms = jnp.mean(o * o, axis=-1, keepdims=True)
        yr_ref[rows, :] = (o * lax.rsqrt(ms + EPS) * _silu(rg_ref[rows, :])).astype(yr_ref.dtype)


def _retention_prompt(rq, rk, rv, rg, tabs):
    S = rq.shape[0]
    H, DK, DV, C = RET_HEADS, RET_DK, RET_DV, RET_CHUNK
    nsub = _pick(S // C, (4, 2, 1))
    T = nsub * C
    dec, cd, kd, gc = tabs
    kern = functools.partial(_ret_prompt_kernel, nsub=nsub, C=C)
    return pl.pallas_call(
        kern,
        grid=(H, S // T),
        in_specs=[
            pl.BlockSpec((T, DK), lambda h, c: (c, h)),
            pl.BlockSpec((T, DK), lambda h, c: (c, h)),
            pl.BlockSpec((T, DV), lambda h, c: (c, h)),
            pl.BlockSpec((T, DV), lambda h, c: (c, h)),
            pl.BlockSpec((1, C, C), lambda h, c: (h, 0, 0)),
            pl.BlockSpec((1, C, DV), lambda h, c: (h, 0, 0)),
            pl.BlockSpec((1, C, DK), lambda h, c: (h, 0, 0)),
            pl.BlockSpec((1, 1, DV), lambda h, c: (h, 0, 0)),
        ],
        out_specs=[
            pl.BlockSpec((T, DV), lambda h, c: (c, h)),
            pl.BlockSpec((1, DK, DV), lambda h, c: (h, 0, 0)),
        ],
        out_shape=[jax.ShapeDtypeStruct((S, H * DV), BF16), jax.ShapeDtypeStruct((H, DK, DV), F32)],
        compiler_params=_cparams("parallel", "arbitrary"),
        name="retention_prompt",
    )(rq, rk, rv, rg, dec, cd, kd, gc)


def _retention_tables(log_gamma):
    H, DK, DV, C = RET_HEADS, RET_DK, RET_DV, RET_CHUNK
    i = jnp.arange(C, dtype=F32)
    rel = i[:, None] - i[None, :]
    causal = rel >= 0
    dec = jnp.where(causal[None], jnp.exp(jnp.where(causal, rel, 0.0)[None] * log_gamma[:, None, None]), 0.0)
    cross = jnp.exp((i[:, None] + 1.0) * log_gamma[None, :])
    kdec = jnp.exp((C - 1.0 - i)[:, None] * log_gamma[None, :])
    gC = jnp.exp(C * log_gamma)
    cd = jnp.broadcast_to(cross.T[:, :, None], (H, C, DV))
    kd = jnp.broadcast_to(kdec.T[:, :, None], (H, C, DK))
    gc = jnp.broadcast_to(gC[:, None, None], (H, 1, DV))
    return dec, cd, kd, gc


def _ret_sample_kernel(st_ref, qt_ref, kt_ref, v_ref, rg_ref, g1_ref, o_ref, ns_ref, *, H):
    qt = qt_ref[0]
    kt = kt_ref[0]
    v = v_ref[0]
    rg = rg_ref[0]
    for h in range(H):
        st = st_ref[0, 0, h]
        qc = qt[:, h:h + 1]
        kc = kt[:, h:h + 1]
        vr = v[h:h + 1, :]
        g1 = g1_ref[h]
        cross = jnp.sum(st * qc, axis=0, keepdims=True)
        qk = jnp.sum(qc * kc, axis=0, keepdims=True)
        o = qk * vr + cross * g1
        ns_ref[0, 0, h] = g1 * st + kc * vr
        ms = jnp.mean(o * o, axis=-1, keepdims=True)
        o_ref[0, h:h + 1, :] = o * lax.rsqrt(ms + EPS) * _silu(rg[h:h + 1, :])


def _retention_sample(state_ret, layer, rq, rk, rv, rg, log_gamma):
    DB = rq.shape[0]
    H, DK, DV = RET_HEADS, RET_DK, RET_DV
    qt = rq.reshape(DB, H, DK).transpose(0, 2, 1)
    kt = rk.reshape(DB, H, DK).transpose(0, 2, 1)
    g1 = jnp.broadcast_to(jnp.exp(1.0 * log_gamma)[:, None, None], (H, 1, DV))
    kern = functools.partial(_ret_sample_kernel, H=H)
    yr, ns = pl.pallas_call(
        kern,
        grid=(DB,),
        in_specs=[
            pl.BlockSpec((1, 1, H, DK, DV), lambda b: (layer, b, 0, 0, 0)),
            pl.BlockSpec((1, DK, H), lambda b: (b, 0, 0)),
            pl.BlockSpec((1, DK, H), lambda b: (b, 0, 0)),
            pl.BlockSpec((1, H, DV), lambda b: (b, 0, 0)),
            pl.BlockSpec((1, H, DV), lambda b: (b, 0, 0)),
            pl.BlockSpec((H, 1, DV), lambda b: (0, 0, 0)),
        ],
        out_specs=[
            pl.BlockSpec((1, H, DV), lambda b: (b, 0, 0)),
            pl.BlockSpec((1, 1, H, DK, DV), lambda b: (0, b, 0, 0, 0)),
        ],
        out_shape=[jax.ShapeDtypeStruct((DB, H, DV), F32),
                   jax.ShapeDtypeStruct((1, DB, H, DK, DV), state_ret.dtype)],
        compiler_params=_cparams("parallel"),
        name="retention_sample",
    )(state_ret, qt, kt, rv.reshape(DB, H, DV), rg.reshape(DB, H, DV), g1)
    return yr.reshape(DB, H * DV), ns[0]


def _sortable_key(x):
    bits = pltpu.bitcast(x, I32)
    return bits ^ ((bits >> 31) & 0x7FFFFFFF)


def _topk_select(keys_ref, nvis, ksel, col_bits, half_ref=None):
    _, R, CW = keys_ref.shape
    ksel_f = float(ksel)

    def lane_fold(f, acc):
        for g in range(CW // LANES):
            acc = acc + f[:, g * LANES:(g + 1) * LANES]
        return acc

    def count(pred):
        body = lambda c, acc: lane_fold(jnp.where(pred(keys_ref[c], c), 1.0, 0.0), acc)
        return jnp.sum(lax.fori_loop(0, nvis, body, jnp.zeros((R, LANES), F32)), axis=1, keepdims=True)

    if half_ref is None:
        def value_bit(bi, u):
            cand_u = u | lax.shift_left(jnp.int32(1), 31 - bi)
            cand = cand_u ^ INT_MIN
            cnt = count(lambda key, c: key >= cand)
            return jnp.where(cnt >= ksel_f, cand_u, u)

        T = lax.fori_loop(0, 32, value_bit, jnp.zeros((R, 1), I32)) ^ INT_MIN
    else:
        I16 = jnp.int16
        HALF = 1 << 15

        def row16(v):
            return jnp.concatenate([jnp.broadcast_to(v, (R, LANES)).astype(I16)] * (CW // LANES), axis=1)

        def count16(pred):
            one, zero = jnp.ones((R, CW), I16), jnp.zeros((R, CW), I16)
            body = lambda c, acc: lane_fold(jnp.where(pred(half_ref[c]), one, zero), acc)
            acc = lax.fori_loop(0, nvis, body, jnp.zeros((R, LANES), I16))
            return jnp.sum(acc.astype(F32), axis=1, keepdims=True)

        def search16(target):
            def bit(bi, u):
                cand_u = u | lax.shift_left(jnp.int32(1), 15 - bi)
                cand = row16(cand_u - HALF)
                cnt = count16(lambda half: half >= cand)
                return jnp.where(cnt >= target, cand_u, u)
            return lax.fori_loop(0, 16, bit, jnp.zeros((R, 1), I32))

        def fill_high(c, carry):
            half_ref[c] = (keys_ref[c] >> 16).astype(I16)
            return carry

        lax.fori_loop(0, nvis, fill_high, 0)
        hi = search16(ksel_f) - HALF
        hi16 = row16(hi)
        above = count16(lambda half: half > hi16)

        def fill_low(c, carry):
            low = ((keys_ref[c] & 0xFFFF) - HALF).astype(I16)
            half_ref[c] = jnp.where(half_ref[c] == hi16, low, jnp.full((R, CW), -HALF, I16))
            return carry

        lax.fori_loop(0, nvis, fill_low, 0)
        T = lax.shift_left(hi, 16) | search16(ksel_f - above)

    need = ksel_f - count(lambda key, c: key > T)
    ties = count(lambda key, c: key == T)

    def col_of(c):
        return c * CW + lax.broadcasted_iota(I32, (R, CW), 1)

    def col_search():
        def col_bit(bi, x):
            cand = x | lax.shift_left(jnp.int32(1), col_bits - 1 - bi)
            cnt = count(lambda key, c: (key == T) & (col_of(c) < cand))
            return jnp.where(cnt < need, cand, x)
        return lax.fori_loop(0, col_bits, col_bit, jnp.zeros((R, 1), I32))

    contested = jnp.max(jnp.where((ties > need) & (T != INT_MIN), 1.0, 0.0)) > 0.0
    J = lax.cond(contested, col_search, lambda: jnp.full((R, 1), 2 ** 30, I32))
    J = jnp.where(T == INT_MIN, -1, J)
    return T, J


def _dsa_select_kernel(iq_ref, iw_ref, ik_ref, bias_ref, keys_ref, half_ref, *, ksel, col_bits):
    NC, QB, CW = keys_ref.shape
    i = pl.program_id(0)
    nvis = ((i + 1) * QB + CW - 1) // CW
    w = iw_ref[...] * (IDX_HEADS ** -0.5 * IDX_DIM ** -0.5)
    rowpos = i * QB + lax.broadcasted_iota(I32, (QB, 1), 0)

    def col_of(c):
        return c * CW + lax.broadcasted_iota(I32, (QB, CW), 1)

    def score_chunk(c, carry):
        kc = ik_ref[c]
        col = c * CW + lax.broadcasted_iota(I32, (Q_BLOCK, CW), 1)
        for r0 in range(0, QB, Q_BLOCK):
            rows = slice(r0, r0 + Q_BLOCK)
            acc = jnp.zeros((Q_BLOCK, CW), F32)
            for h in range(IDX_HEADS):
                s = _dot_nt(iq_ref[rows, h * IDX_DIM:(h + 1) * IDX_DIM], kc)
                acc = acc + jnp.maximum(s, 0.0) * w[rows, h:h + 1]
            keys_ref[c, rows, :] = jnp.where(col <= rowpos[rows], _sortable_key(acc), INT_MIN)
        return carry

    lax.fori_loop(0, nvis, score_chunk, 0)
    T, J = _topk_select(keys_ref, nvis, ksel, col_bits, half_ref)

    def emit(c, carry):
        key = keys_ref[c]
        col = col_of(c)
        sel = ((key > T) | ((key == T) & (col <= J))) & (col <= rowpos)
        bias_ref[0, c] = jnp.where(sel, 0.0, NEG_BIAS).astype(bias_ref.dtype)
        return carry

    lax.fori_loop(0, nvis, emit, 0)

    def fill(c, carry):
        bias_ref[0, c] = jnp.full((QB, CW), NEG_BIAS, bias_ref.dtype)
        return carry

    lax.fori_loop(nvis, NC, fill, 0)


def _dsa_select_prompt(iq, iw, ik, ksel, QB, CW):
    S = iq.shape[0]
    NQ, NC = S // QB, S // CW
    kern = functools.partial(_dsa_select_kernel, ksel=ksel, col_bits=int(np.ceil(np.log2(S))))
    return pl.pallas_call(
        kern,
        grid=(NQ,),
        in_specs=[
            pl.BlockSpec((QB, iq.shape[1]), lambda i: (i, 0)),
            pl.BlockSpec((QB, iw.shape[1]), lambda i: (i, 0)),
            pl.BlockSpec((NC, CW, ik.shape[1]), lambda i: (0, 0, 0)),
        ],
        out_specs=pl.BlockSpec((1, NC, QB, CW), lambda i: (i, 0, 0, 0)),
        out_shape=jax.ShapeDtypeStruct((NQ, NC, QB, CW), BF16),
        scratch_shapes=[pltpu.VMEM((NC, QB, CW), I32), pltpu.VMEM((NC, QB, CW), jnp.int16)],
        compiler_params=_cparams("parallel"),
        name="dsa_select_prompt",
    )(iq, iw, ik.reshape(NC, CW, ik.shape[1]))


def _dsa_attn_kernel(qi_ref, kj_ref, fin_ref, q_ref, k_ref, v_ref, b_ref, o_ref, m_sc, l_sc, acc_sc,
                     *, H, DH, TK, SUB):
    step = pl.program_id(0)

    @pl.when(kj_ref[step] == 0)
    def _():
        m_sc[...] = jnp.full(m_sc.shape, NEG_BIAS, F32)
        l_sc[...] = jnp.zeros_like(l_sc)
        acc_sc[...] = jnp.zeros_like(acc_sc)

    ones = jnp.ones((SUB, LANES), BF16)
    for t in range(TK // SUB):
        keys = slice(t * SUB, (t + 1) * SUB)
        for h in range(H):
            cols = slice(h * DH, (h + 1) * DH)
            s = _dot_nt(q_ref[:, cols], k_ref[keys, cols]).astype(BF16) + b_ref[0, 0, :, keys]
            m_prev = m_sc[h]
            m_new = jnp.maximum(m_prev, jnp.max(s, axis=-1, keepdims=True).astype(F32))
            alpha = jnp.exp2(m_prev - m_new)
            p = jnp.exp2(s - jnp.concatenate([m_new.astype(BF16)] * (SUB // LANES), axis=1))
            pv = _dot(p, jnp.concatenate([v_ref[keys, cols], ones], axis=1))
            l_sc[h] = alpha * l_sc[h] + pv[:, DH:]
            acc_sc[:, cols] = alpha * acc_sc[:, cols] + pv[:, :DH]
            m_sc[h] = m_new

    @pl.when(fin_ref[step] == 1)
    def _():
        for h in range(H):
            cols = slice(h * DH, (h + 1) * DH)
            o_ref[:, cols] = (acc_sc[:, cols] / l_sc[h]).astype(o_ref.dtype)


def _dsa_attn_prompt(aq, ak, av, bias, QB, TK):
    S, W = aq.shape
    H, DH = ATT_HEADS, ATT_DH
    assert DH == LANES
    bq = bias.shape[2] // QB
    pairs = [(i, j) for i in range(S // QB) for j in range(((i + 1) * QB - 1) // TK + 1)]
    qi = jnp.asarray([p[0] for p in pairs], I32)
    kj = jnp.asarray([p[1] for p in pairs], I32)
    fin = jnp.asarray([int(p[1] == ((p[0] + 1) * QB - 1) // TK) for p in pairs], I32)
    kern = functools.partial(_dsa_attn_kernel, H=H, DH=DH, TK=TK, SUB=_pick(TK, (512, 256, 128)))
    return pl.pallas_call(
        kern,
        grid_spec=pltpu.PrefetchScalarGridSpec(
            num_scalar_prefetch=3,
            grid=(len(pairs),),
            in_specs=[
                pl.BlockSpec((QB, W), lambda s, qi, kj, fin: (qi[s], 0)),
                pl.BlockSpec((TK, W), lambda s, qi, kj, fin: (kj[s], 0)),
                pl.BlockSpec((TK, W), lambda s, qi, kj, fin: (kj[s], 0)),
                pl.BlockSpec((1, 1, QB, TK), lambda s, qi, kj, fin: (qi[s] // bq, kj[s], qi[s] % bq, 0)),
            ],
            out_specs=pl.BlockSpec((QB, W), lambda s, qi, kj, fin: (qi[s], 0)),
            scratch_shapes=[pltpu.VMEM((H, QB, LANES), F32), pltpu.VMEM((H, QB, LANES), F32),
                            pltpu.VMEM((QB, W), F32)],
        ),
        out_shape=jax.ShapeDtypeStruct((S, W), BF16),
        compiler_params=_cparams("arbitrary"),
        name="dsa_attn_prompt",
    )(qi, kj, fin, aq, ak, av, bias)


def _idx_sample_kernel(pt_ref, iq_ref, w_ref, *refs):
    page_refs, o_ref = refs[:-1], refs[-1]
    iq = iq_ref[0].astype(BF16)
    w = w_ref[0]
    rows = []
    for page_ref in page_refs:
        s = _dot_nt(iq, page_ref[0, 0].astype(BF16))
        rows.append(jnp.sum(jnp.maximum(s, 0.0) * w, axis=0, keepdims=True))
    o_ref[0, 0] = jnp.concatenate(rows, axis=1)


def _idx_sample(cache_idx_k, layer, page_table, iq, iw):
    DB, NP = page_table.shape
    P, Di = cache_idx_k.shape[2:]
    Hi = IDX_HEADS
    G = _pick(NP, (16, 8, 4, 2, 1))
    w = jnp.broadcast_to((iw * (IDX_HEADS ** -0.5 * IDX_DIM ** -0.5))[:, :, None], (DB, Hi, P))

    def page_spec(r):
        return pl.BlockSpec((1, 1, P, Di), lambda b, g, pt: (layer, pt[b, g * G + r], 0, 0))

    out = pl.pallas_call(
        _idx_sample_kernel,
        grid_spec=pltpu.PrefetchScalarGridSpec(
            num_scalar_prefetch=1,
            grid=(DB, NP // G),
            in_specs=[
                pl.BlockSpec((1, Hi, Di), lambda b, g, pt: (b, 0, 0)),
                pl.BlockSpec((1, Hi, P), lambda b, g, pt: (b, 0, 0)),
            ] + [page_spec(r) for r in range(G)],
            out_specs=pl.BlockSpec((1, 1, 1, G * P), lambda b, g, pt: (b, g, 0, 0)),
        ),
        out_shape=jax.ShapeDtypeStruct((DB, NP // G, 1, G * P), F32),
        compiler_params=_cparams("parallel", "arbitrary"),
        name="dsa_index_sample",
    )(page_table, iq.reshape(DB, Hi, Di), w, *([cache_idx_k] * G))
    return out.reshape(DB, NP * P)


def _select_sample_kernel(sc_ref, iq_ref, ikn_ref, iw_ref, idx_ref, keys_ref, cnt_ref, *, ksel, past_len,
                          col_bits):
    NC, R, CW = keys_ref.shape
    NP = NC - 1
    w = iw_ref[...] * (IDX_HEADS ** -0.5 * IDX_DIM ** -0.5)
    ikn = ikn_ref[...]
    new = jnp.zeros((R, 1), F32)
    for h in range(IDX_HEADS):
        s = jnp.sum(iq_ref[:, h * IDX_DIM:(h + 1) * IDX_DIM] * ikn, axis=-1, keepdims=True)
        new = new + jnp.maximum(s, 0.0) * w[:, h:h + 1]

    def col_of(c):
        return c * CW + lax.broadcasted_iota(I32, (R, CW), 1)

    def load(c, carry):
        keys_ref[c] = _sortable_key(sc_ref[c] + 0.0)
        return carry

    lax.fori_loop(0, NP, load, 0)
    keys_ref[NP] = jnp.where(col_of(NP) <= past_len, _sortable_key(jnp.broadcast_to(new, (R, CW))), INT_MIN)
    T, J = _topk_select(keys_ref, NC, ksel, col_bits)

    tri = (lax.broadcasted_iota(I32, (CW, CW), 0) <= lax.broadcasted_iota(I32, (CW, CW), 1)).astype(BF16)

    def prefix(c, before):
        key = keys_ref[c]
        col = col_of(c)
        sel = ((key > T) | ((key == T) & (col <= J))) & (col <= past_len)
        f = jnp.where(sel, 1.0, 0.0)
        cnt_ref[c] = _dot(f.astype(BF16), tri) + before
        return before + jnp.sum(f, axis=-1, keepdims=True)

    lax.fori_loop(0, NC, prefix, jnp.zeros((R, 1), F32))

    jcol = lax.broadcasted_iota(I32, (ksel, 1), 0).astype(F32)
    lane = lax.broadcasted_iota(I32, (ksel, LANES), 1)

    def one_row(r, out):
        def chunk(c, acc):
            return acc + jnp.where(cnt_ref[c, pl.ds(r, 1), :] <= jcol, 1.0, 0.0)
        acc = lax.fori_loop(0, NC, chunk, jnp.zeros((ksel, CW), F32))
        return jnp.where(lane == r, jnp.sum(acc, axis=-1, keepdims=True), out)

    idx_ref[...] = lax.fori_loop(0, R, one_row, jnp.zeros((ksel, LANES), F32)).astype(I32)


def _select_sample(scores, iq, ik_new, iw, ksel):
    DB = scores.shape[0]
    P = LANES
    NP = scores.shape[1] // P
    past_len = NP * P
    assert DB <= LANES
    sc = scores.reshape(DB, NP, P).transpose(1, 0, 2)
    kern = functools.partial(_select_sample_kernel, ksel=ksel, past_len=past_len,
                             col_bits=int(np.ceil(np.log2(past_len + P))))
    idx_t = pl.pallas_call(
        kern,
        out_shape=jax.ShapeDtypeStruct((ksel, LANES), I32),
        scratch_shapes=[pltpu.VMEM((NP + 1, DB, P), I32), pltpu.VMEM((NP + 1, DB, P), F32)],
        compiler_params=pltpu.CompilerParams(vmem_limit_bytes=VMEM_LIMIT_BYTES),
        name="dsa_select_sample",
    )(sc, iq, ik_new, iw)
    return idx_t[:, :DB].T


def _attn_sample_kernel(idx_ref, pt_ref, q_ref, kn_ref, vn_ref, ck_hbm, cv_hbm, o_ref, kbuf, vbuf, sem,
                        m_sc, l_sc, acc_sc, *, layer, G, ksel, past_len, page):
    b = pl.program_id(0)
    slot = b % 2
    scale = q_ref.shape[-1] ** -0.5

    def row_copies(seq, r, sl):
        pos = jnp.minimum(idx_ref[seq * ksel + r], past_len - 1)
        src = (layer, pt_ref[seq, pos // page], pos % page)
        return (pltpu.make_async_copy(ck_hbm.at[src], kbuf.at[sl, r], sem.at[sl, 0]),
                pltpu.make_async_copy(cv_hbm.at[src], vbuf.at[sl, r], sem.at[sl, 1]))

    def start_rows(seq, sl):
        def body(r, carry):
            for cp in row_copies(seq, r, sl):
                cp.start()
            return carry
        lax.fori_loop(0, ksel, body, 0)

    @pl.when(b == 0)
    def _():
        start_rows(0, 0)

    @pl.when(b + 1 < pl.num_programs(0))
    def _():
        start_rows(b + 1, 1 - slot)

    def wait_row(r, carry):
        for cp in row_copies(b, r, slot):
            cp.wait()
        return carry

    lax.fori_loop(0, ksel, wait_row, 0)

    m_sc[...] = jnp.full(m_sc.shape, NEG_BIAS, F32)
    l_sc[...] = jnp.zeros_like(l_sc)
    acc_sc[...] = jnp.zeros_like(acc_sc)
    q = q_ref[0]

    def slot_bias(r):
        pos = idx_ref[b * ksel + r]
        return jnp.full((1, 1, 1), jnp.where(pos >= past_len, NEG_BIAS, 0.0), F32)

    def update(kk, vv, bias3):
        s = jnp.sum(kk * q[None], axis=-1, keepdims=True) * scale + bias3
        m_prev = m_sc[...]
        m_new = jnp.maximum(m_prev, jnp.max(s, axis=0))
        alpha = jnp.exp(m_prev - m_new)
        pr = jnp.exp(s - m_new[None])
        l_sc[...] = alpha * l_sc[...] + jnp.sum(pr, axis=0)
        acc_sc[...] = alpha * acc_sc[...] + jnp.sum(pr * vv, axis=0)
        m_sc[...] = m_new

    for c in range(ksel // G):
        rows = pl.ds(c * G, G)
        update(kbuf[slot, rows], vbuf[slot, rows],
               jnp.concatenate([slot_bias(c * G + r) for r in range(G)], axis=0))

    last = idx_ref[b * ksel + ksel - 1]
    update(kn_ref[...], vn_ref[...], jnp.full((1, 1, 1), jnp.where(last >= past_len, 0.0, NEG_BIAS), F32))
    o_ref[0] = acc_sc[...] / l_sc[...]


def _attn_sample(cache_k, cache_v, layer, page_table, aq, ak_new, av_new, idx):
    DB, NP = page_table.shape
    P, H, DH = cache_k.shape[2:]
    ksel = idx.shape[1]
    vec_spec = pl.BlockSpec((1, H, DH), lambda b, idx_ref, pt: (b, 0, 0))
    kern = functools.partial(_attn_sample_kernel, layer=layer, G=_pick(ksel, (32, 16, 8, 4, 2, 1)), ksel=ksel,
                             past_len=NP * P, page=P)
    out = pl.pallas_call(
        kern,
        grid_spec=pltpu.PrefetchScalarGridSpec(
            num_scalar_prefetch=2,
            grid=(DB,),
            in_specs=[vec_spec, vec_spec, vec_spec, pl.BlockSpec(memory_space=pl.ANY),
                      pl.BlockSpec(memory_space=pl.ANY)],
            out_specs=vec_spec,
            scratch_shapes=[pltpu.VMEM((2, ksel, H, DH), F32), pltpu.VMEM((2, ksel, H, DH), F32),
                            pltpu.SemaphoreType.DMA((2, 2)),
                            pltpu.VMEM((H, 1), F32), pltpu.VMEM((H, 1), F32), pltpu.VMEM((H, DH), F32)],
        ),
        out_shape=jax.ShapeDtypeStruct((DB, H, DH), F32),
        compiler_params=_cparams("arbitrary"),
        name="dsa_attn_sample",
    )(idx.reshape(-1), page_table, aq.reshape(DB, H, DH), ak_new.reshape(DB, H, DH), av_new.reshape(DB, H, DH),
      cache_k, cache_v)
    return out.reshape(DB, H * DH)


def _rotary_tables(pos):
    d = RET_DK
    inv_freq = 1.0 / (ROPE_BASE ** jnp.linspace(0.0, 1.0, d // 2, dtype=F32))
    ang = pos.astype(F32)[:, None] * inv_freq[None, :]
    cos = jnp.repeat(jnp.cos(ang), 2, axis=1)
    sin = jnp.sin(ang)
    sin_signed = jnp.stack([-sin, sin], axis=-1).reshape(pos.shape[0], d)
    return cos, sin_signed


def _in_proj_columns(w_in):
    rqk, rv_w = RET_HEADS * RET_DK, RET_HEADS * RET_DV
    aw, iqw = ATT_HEADS * ATT_DH, IDX_HEADS * IDX_DIM
    d_model = w_in.shape[1]
    names = ("rq", "rk", "rv", "rg", "aq", "ak", "av", "iq", "ik", "iw", "ga", "gb")
    sizes = (rqk, rqk, rv_w, rv_w, aw, aw, aw, iqw, IDX_DIM, IDX_HEADS, d_model, d_model)
    offs = np.concatenate([[0], np.cumsum(sizes)]).tolist()
    assert offs[-1] == w_in.shape[2]
    return {n: (offs[t], sizes[t]) for t, n in enumerate(names)}


def _mix_inputs(hp, hs, w_in_t, layer, cols, qg, kg, rot_p, rot_s, q_scale):
    tm = _pick(hp.shape[0], ROW_TILES)
    P, Sm = {}, {}

    def proj(name, epi_p, ex_p, out_p, epi_s, ex_s, out_s, tn_prefs=(512, 256, 128), n=None):
        col0 = cols[name][0]
        n = n or cols[name][1]
        outs = _mm([hp], [(0, w_in_t, layer, col0, n, True)], ex_p, epi_p, [d for _, d in out_p], tm=tm,
                   tn=_pick(n, tn_prefs), name="proj_" + name, rider=([hs], ex_s, epi_s, [d for _, d in out_s]))
        for (key, _), o in zip(out_p, outs):
            P[key] = o
        for (key, _), o in zip(out_s, outs[len(out_p):]):
            Sm[key] = o

    def same(name, dt_p, **kw):
        proj(name, _epi_identity, [], [(name, dt_p)], _epi_identity, [], [(name, F32)], **kw)

    rot_p = [(t, "rows") for t in rot_p]
    rot_s = [(t, "rows") for t in rot_s]
    for name, scale in (("rq", 1.0), ("rk", RET_DK ** -0.5)):
        epi = functools.partial(_epi_rotary, scale)
        proj(name, epi, rot_p, [(name, BF16)], epi, rot_s, [(name, F32)])
    same("rv", BF16)
    same("rg", F32)
    norm1, norm2 = functools.partial(_epi_head_norm, 1), functools.partial(_epi_head_norm, 2)
    proj("aq", norm1, [(qg.reshape(1, -1) * q_scale, "full")], [("aq", BF16)],
         norm1, [(qg.reshape(1, -1), "full")], [("aq", F32)])
    kgain = [(kg.reshape(1, -1), "full")]
    proj("ak", norm2, kgain, [("ak", F32), ("ak16", BF16)], norm1, kgain, [("ak", F32)])
    proj("av", _epi_two_copies, [], [("av", F32), ("av16", BF16)], _epi_identity, [], [("av", F32)])
    same("iq", BF16)
    same("ik", F32, tn_prefs=(2 * LANES,), n=2 * LANES)
    same("ga", F32)
    same("gb", F32)
    for d in (P, Sm):
        ikw = d["ik"]
        d["ik"] = ikw[:, :IDX_DIM]
        d["iw"] = ikw[:, IDX_DIM:IDX_DIM + IDX_HEADS]
    return P, Sm


def _merge_ffn(xp, xs, mp, ms, layer, w_ret_o, w_att_o, w_out, g_ffn, w_gate, w_up, wd):
    M, D = xp.shape
    F = w_gate.shape[2]
    tm = _pick(M, ROW_TILES)
    gates = lambda t: [(t[2], "tile"), (t[3], "tile")]
    m_p, m_s = _mm(list(mp[:2]), [(0, w_ret_o, layer, 0, D, False), (1, w_att_o, layer, 0, D, False)], gates(mp),
                   _epi_merge, [BF16], tm=_pick(M, ROW_TILES[1:]), tn=_pick(D, (512, 256, 128)), name="merge",
                   rider=(list(ms[:2]), gates(ms), _epi_merge, [BF16]))
    x1p, x1s = _mm([m_p], [(0, w_out, layer, 0, D, False)], [(xp, "tile")], _epi_residual, [F32], tm=tm,
                   tn=_pick(D, (512, 256, 128)), name="out_proj",
                   rider=([m_s], [(xs, "tile")], _epi_residual, [F32]))
    hfp, hfs = _rmsnorm(x1p, g_ffn), _rmsnorm(x1s, g_ffn)
    up, us = _mm([hfp], [(0, w_gate, layer, 0, F, False), (0, w_up, layer, 0, F, False)], [], _epi_swiglu, [BF16],
                 tm=tm, tn=_pick(F, (256, 128)), name="ffn_up", rider=([hfs], [], _epi_swiglu, [BF16]))
    down = lambda u, x1: _mm([u], [(0, wd, None, 0, D, False)], [(x1, "tile")], _epi_residual, [F32],
                             tm=_pick(u.shape[0], ROW_TILES[1:]), tn=_pick(D, (256, 128)), name="ffn_down",
                             hold_b=False)[0]
    return down(up, x1p), down(us, x1s)


def kernel(x_prompt, x_sample, cache_k, cache_v, cache_idx_k, state_ret, page_table, norm_mix_g, w_in,
           q_norm_g, k_norm_g, w_ret_o, w_att_o, w_out, norm_ffn_g, w_ffn_gate, w_ffn_up, w_ffn_down):
    depth = w_in.shape[0]
    B, S, D = x_prompt.shape
    DB, T, _ = x_sample.shape
    assert B == 1 and T == 1
    page_size = cache_k.shape[2]
    past_len = page_table.shape[1] * page_size

    log_gamma = jnp.log1p(-jnp.exp2(-5.0 - jnp.arange(RET_HEADS, dtype=F32)))
    ret_tabs = _retention_tables(log_gamma)
    cos_p, sin_p = _rotary_tables(jnp.arange(S, dtype=I32))
    cos_s, sin_s = _rotary_tables(jnp.full((DB,), past_len, I32))
    ksel_p = min(TOPK_MAX, S // 4)
    ksel_s = min(TOPK_MAX, (past_len + T) // 4)
    cw = _pick(S, (512, 256, 128))
    qb = Q_BLOCK
    cols = _in_proj_columns(w_in)
    w_in_t = jnp.swapaxes(w_in, 1, 2)

    xp = x_prompt.reshape(S, D)
    xs = x_sample.reshape(DB, D)
    kp, vp, ikp, stp, ksm, vsm, iks, sts = [], [], [], [], [], [], [], []
    for layer in range(depth):
        wd = w_ffn_down[layer].astype(BF16)
        dense = (layer, w_ret_o, w_att_o, w_out, norm_ffn_g[layer], w_ffn_gate, w_ffn_up, wd)

        p, s = _mix_inputs(_rmsnorm(xp, norm_mix_g[layer]), _rmsnorm(xs, norm_mix_g[layer]), w_in_t, layer, cols,
                           q_norm_g[layer], k_norm_g[layer], (cos_p, sin_p), (cos_s, sin_s),
                           ATT_DH ** -0.5 * float(np.log2(np.e)))

        yr_p, st_p = _retention_prompt(p["rq"], p["rk"], p["rv"], p["rg"], ret_tabs)
        bias = _dsa_select_prompt(p["iq"], p["iw"], p["ik"].astype(BF16), ksel_p, _pick(S, (2 * qb, qb)), cw)
        ya_p = _dsa_attn_prompt(p["aq"], p["ak16"], p["av16"], bias, qb, cw)

        yr_s, st_s = _retention_sample(state_ret, layer, s["rq"], s["rk"], s["rv"], s["rg"], log_gamma)
        scores = _idx_sample(cache_idx_k, layer, page_table, s["iq"], s["iw"])
        sel_idx = _select_sample(scores, s["iq"], s["ik"], s["iw"], ksel_s)
        ya_s = _attn_sample(cache_k, cache_v, layer, page_table, s["aq"], s["ak"], s["av"], sel_idx)

        xp, xs = _merge_ffn(xp, xs, (yr_p, ya_p, p["ga"], p["gb"]),
                            (yr_s.astype(BF16), ya_s.astype(BF16), s["ga"], s["gb"]), *dense)
        kp.append(p["ak"].reshape(B, S, ATT_HEADS, ATT_DH))
        vp.append(p["av"].reshape(B, S, ATT_HEADS, ATT_DH))
        ikp.append(p["ik"].reshape(B, S, IDX_DIM))
        stp.append(st_p[None].astype(state_ret.dtype))
        ksm.append(s["ak"].reshape(DB, T, ATT_HEADS, ATT_DH))
        vsm.append(s["av"].reshape(DB, T, ATT_HEADS, ATT_DH))
        iks.append(s["ik"].reshape(DB, T, IDX_DIM))
        sts.append(st_s)
    return (xp.reshape(B, S, D), xs.reshape(DB, T, D), jnp.stack(kp), jnp.stack(vp), jnp.stack(ikp),
            jnp.stack(stp), jnp.stack(ksm), jnp.stack(vsm), jnp.stack(iks), jnp.stack(sts))
```

```python
import functools

import numpy as np
import jax
import jax.numpy as jnp
from jax import lax
from jax.experimental import pallas as pl
from jax.experimental.pallas import tpu as pltpu

F32 = jnp.float32
BF16 = jnp.bfloat16
I32 = jnp.int32

RET_HEADS = 8
RET_DK = 256
RET_DV = 512
RET_CHUNK = 128
ROPE_BASE = 10000.0
ATT_HEADS = 16
ATT_DH = 128
IDX_HEADS = 8
IDX_DIM = 128
TOPK_MAX = 256
Q_BLOCK = 128
EPS = 1e-6

LANES = 128
VMEM_LIMIT_BYTES = 56 * 1024 * 1024

ROW_TILES = (1024, 512, 256, 128, 64, 32, 16, 8)

NEG_BIAS = -1e30
INT_MIN = -(2 ** 31)


def _cparams(*sem):
    return pltpu.CompilerParams(dimension_semantics=sem, vmem_limit_bytes=VMEM_LIMIT_BYTES)


def _pick(n, prefs):
    for p in prefs:
        if p <= n and n % p == 0:
            return p
    return n


def _dot(a, b):
    return jnp.dot(a, b, preferred_element_type=F32)


def _dot_nt(a, b):
    return lax.dot_general(a, b, (((1,), (1,)), ((), ())), preferred_element_type=F32)


def _sigmoid(x):
    return 1.0 / (1.0 + jnp.exp(-x))


def _silu(x):
    return x * _sigmoid(x)


def _rmsnorm_kernel(x_ref, g_ref, o_ref):
    x = x_ref[...]
    ms = jnp.mean(x * x, axis=-1, keepdims=True)
    o_ref[...] = (x * lax.rsqrt(ms + EPS) * g_ref[...]).astype(o_ref.dtype)


def _rmsnorm(x, g):
    M, D = x.shape
    tm = _pick(M, (256, 128, 64, 32, 16, 8))
    return pl.pallas_call(
        _rmsnorm_kernel,
        grid=(M // tm,),
        in_specs=[pl.BlockSpec((tm, D), lambda i: (i, 0)), pl.BlockSpec((1, D), lambda i: (0, 0))],
        out_specs=pl.BlockSpec((tm, D), lambda i: (i, 0)),
        out_shape=jax.ShapeDtypeStruct((M, D), BF16),
        compiler_params=_cparams("parallel"),
        name="rmsnorm",
    )(x, g.reshape(1, D))


def _mm_kernel(*refs, n_a, b_src, b_nt, n_e, n_o, cast, epilogue, rider):
    n_b = len(b_src)
    take = lambda n, it=iter(refs): [next(it) for _ in range(n)]
    a_refs, b_refs, e_refs = take(n_a), take(n_b), take(n_e)
    if rider:
        n_e2, n_o2, epilogue2 = rider
        a2_refs, e2_refs = take(n_a), take(n_e2)
    o_refs = take(n_o)
    if rider:
        o2_refs = take(n_o2)
    if cast:
        w_refs = take(n_b)

    def compute(a_rs, e_rs, o_rs, epi):
        bs = w_refs if cast else b_refs
        accs = [(_dot_nt if nt else _dot)(a_rs[s][...], b[...]) for s, nt, b in zip(b_src, b_nt, bs)]
        for o, r in zip(o_rs, epi(accs, [e[...] for e in e_rs])):
            o[...] = r.astype(o.dtype)

    if cast or rider:
        @pl.when(pl.program_id(1) == 0)
        def _():
            if cast:
                for b, w in zip(b_refs, w_refs):
                    w[...] = b[...].reshape(w.shape).astype(w.dtype)
            if rider:
                compute(a2_refs, e2_refs, o2_refs, epilogue2)
    compute(a_refs, e_refs, o_refs, epilogue)


def _mm(a_list, b_list, extras, epilogue, out_dtypes, *, tm, tn, name, hold_b=True, rider=None):
    M = a_list[0].shape[0]
    N = b_list[0][4]
    cast = b_list[0][2] is not None
    assert all((b[2] is not None) == cast for b in b_list) and (hold_b or not (cast or rider))
    if hold_b:
        grid = (N // tn, M // tm)
        ij = lambda f: (lambda j, i: f(i, j))
    else:
        grid = (M // tm, N // tn)
        ij = lambda f: f
    in_specs, args, scratch = [], [], []
    for a in a_list:
        in_specs.append(pl.BlockSpec((tm, a.shape[1]), ij(lambda i, j: (i, 0))))
        args.append(a)
    for _, w, layer, col0, _, nt in b_list:
        if nt:
            K = w.shape[2]
            if col0 % tn == 0:
                spec = pl.BlockSpec((None, tn, K), ij(lambda i, j, l=layer, c=col0 // tn: (l, c + j, 0)))
            else:
                spec = pl.BlockSpec((pl.Element(1), pl.Element(tn), pl.Element(K)),
                                    ij(lambda i, j, l=layer, c=col0: (l, pl.multiple_of(c + j * tn, 8), 0)))
            scratch.append(pltpu.VMEM((tn, K), BF16))
        elif cast:
            assert col0 % tn == 0
            K = w.shape[1]
            spec = pl.BlockSpec((None, K, tn), ij(lambda i, j, l=layer, c=col0 // tn: (l, 0, c + j)))
            scratch.append(pltpu.VMEM((K, tn), BF16))
        else:
            assert col0 % tn == 0
            spec = pl.BlockSpec((w.shape[0], tn), ij(lambda i, j, c=col0 // tn: (0, c + j)))
        in_specs.append(spec)
        args.append(w)
    def add_extras(ex_list, rows, row_index):
        for arr, kind in ex_list:
            if kind == "tile":
                in_specs.append(pl.BlockSpec((rows, tn), ij(lambda i, j: (row_index(i), j))))
            elif kind == "rows":
                in_specs.append(pl.BlockSpec((rows, arr.shape[1]), ij(lambda i, j: (row_index(i), 0))))
            else:
                in_specs.append(pl.BlockSpec(arr.shape, ij(lambda i, j, nd=arr.ndim: (0,) * nd)))
            args.append(arr)

    add_extras(extras, tm, lambda i: i)
    out_specs = [pl.BlockSpec((tm, tn), ij(lambda i, j: (i, j))) for _ in out_dtypes]
    out_shape = [jax.ShapeDtypeStruct((M, N), od) for od in out_dtypes]
    if rider:
        a_list2, extras2, epilogue2, out_dtypes2 = rider
        M2 = a_list2[0].shape[0]
        for a in a_list2:
            in_specs.append(pl.BlockSpec((M2, a.shape[1]), ij(lambda i, j: (0, 0))))
            args.append(a)
        add_extras(extras2, M2, lambda i: 0)
        out_specs += [pl.BlockSpec((M2, tn), ij(lambda i, j: (0, j))) for _ in out_dtypes2]
        out_shape += [jax.ShapeDtypeStruct((M2, N), od) for od in out_dtypes2]
        rider = (len(extras2), len(out_dtypes2), epilogue2)
    kern = functools.partial(_mm_kernel, n_a=len(a_list), b_src=tuple(b[0] for b in b_list),
                             b_nt=tuple(b[5] for b in b_list), n_e=len(extras), n_o=len(out_dtypes), cast=cast,
                             epilogue=epilogue, rider=rider)
    outs = pl.pallas_call(
        kern,
        grid=grid,
        in_specs=in_specs,
        out_specs=out_specs,
        out_shape=out_shape,
        scratch_shapes=scratch,
        compiler_params=_cparams("parallel", "arbitrary"),
        name=name,
    )(*args)
    return outs


def _epi_identity(accs, extras):
    return [accs[0]]


def _epi_two_copies(accs, extras):
    return [accs[0], accs[0]]


def _epi_rotary(scale, accs, extras):
    acc = accs[0]
    cos, sin_signed = extras
    dk = cos.shape[1]
    lane = lax.broadcasted_iota(I32, (acc.shape[0], dk), 1)
    even = (lane % 2) == 0
    outs = []
    for h in range(acc.shape[1] // dk):
        x = acc[:, h * dk:(h + 1) * dk]
        swapped = jnp.where(even, pltpu.roll(x, dk - 1, 1), pltpu.roll(x, 1, 1))
        outs.append((x * cos + swapped * sin_signed) * scale)
    return [jnp.concatenate(outs, axis=1) if len(outs) > 1 else outs[0]]


def _epi_head_norm(n_out, accs, extras):
    acc = accs[0]
    g = extras[0]
    dh = g.shape[1]
    outs = []
    for h in range(acc.shape[1] // dh):
        x = acc[:, h * dh:(h + 1) * dh]
        ms = jnp.mean(x * x, axis=-1, keepdims=True)
        outs.append(x * lax.rsqrt(ms + EPS) * g)
    y = jnp.concatenate(outs, axis=1) if len(outs) > 1 else outs[0]
    return [y] * n_out


def _epi_merge(accs, extras):
    ga, gb = extras
    return [_sigmoid(ga) * accs[0] + _sigmoid(gb) * accs[1]]


def _epi_residual(accs, extras):
    return [extras[0] + accs[0]]


def _epi_swiglu(accs, extras):
    return [_silu(accs[0]) * accs[1]]


def _ret_prompt_kernel(q_ref, k_ref, v_ref, rg_ref, dec_ref, cd_ref, kd_ref, gc_ref, yr_ref, st_ref,
                       *, nsub, C):
    @pl.when(pl.program_id(1) == 0)
    def _():
        st_ref[...] = jnp.zeros_like(st_ref)

    DK, DV = kd_ref.shape[2], cd_ref.shape[2]
    for t in range(nsub):
        rows = slice(t * C, (t + 1) * C)
        for g in range(st_ref.shape[0]):
            qk = slice(g * DK, (g + 1) * DK)
            vc = slice(g * DV, (g + 1) * DV)
            q = q_ref[rows, qk]
            k = k_ref[rows, qk]
            v = v_ref[rows, vc]
            st = st_ref[g]
            s = _dot_nt(q, k) * dec_ref[g]
            o = _dot(s.astype(BF16), v) + _dot(q, st.astype(BF16)) * cd_ref[g]
            kdt = (k.astype(F32) * kd_ref[g]).T.astype(BF16)
            st_ref[g] = gc_ref[g] * st + _dot(kdt, v)
            ms = jnp.mean(o * o, axis=-1, keepdims=True)
            yr_ref[rows, vc] = (o * lax.rsqrt(ms + EPS) * _silu(rg_ref[rows, vc])).astype(yr_ref.dtype)


def _retention_prompt(rq, rk, rv, rg, tabs):
    S = rq.shape[0]
    H, DK, DV, C = RET_HEADS, RET_DK, RET_DV, RET_CHUNK
    nsub = _pick(S // C, (4, 2, 1))
    T = nsub * C
    HB = _pick(H, (4, 2, 1))
    dec, cd, kd, gc = tabs
    kern = functools.partial(_ret_prompt_kernel, nsub=nsub, C=C)
    return pl.pallas_call(
        kern,
        grid=(H // HB, S // T),
        in_specs=[
            pl.BlockSpec((T, HB * DK), lambda h, c: (c, h)),
            pl.BlockSpec((T, HB * DK), lambda h, c: (c, h)),
            pl.BlockSpec((T, HB * DV), lambda h, c: (c, h)),
            pl.BlockSpec((T, HB * DV), lambda h, c: (c, h)),
            pl.BlockSpec((HB, C, C), lambda h, c: (h, 0, 0)),
            pl.BlockSpec((HB, C, DV), lambda h, c: (h, 0, 0)),
            pl.BlockSpec((HB, C, DK), lambda h, c: (h, 0, 0)),
            pl.BlockSpec((HB, 1, DV), lambda h, c: (h, 0, 0)),
        ],
        out_specs=[
            pl.BlockSpec((T, HB * DV), lambda h, c: (c, h)),
            pl.BlockSpec((HB, DK, DV), lambda h, c: (h, 0, 0)),
        ],
        out_shape=[jax.ShapeDtypeStruct((S, H * DV), BF16), jax.ShapeDtypeStruct((H, DK, DV), F32)],
        compiler_params=_cparams("parallel", "arbitrary"),
        name="retention_prompt",
    )(rq, rk, rv, rg, dec, cd, kd, gc)


def _retention_tables(log_gamma):
    H, DK, DV, C = RET_HEADS, RET_DK, RET_DV, RET_CHUNK
    i = jnp.arange(C, dtype=F32)
    rel = i[:, None] - i[None, :]
    causal = rel >= 0
    dec = jnp.where(causal[None], jnp.exp(jnp.where(causal, rel, 0.0)[None] * log_gamma[:, None, None]), 0.0)
    cross = jnp.exp((i[:, None] + 1.0) * log_gamma[None, :])
    kdec = jnp.exp((C - 1.0 - i)[:, None] * log_gamma[None, :])
    gC = jnp.exp(C * log_gamma)
    cd = jnp.broadcast_to(cross.T[:, :, None], (H, C, DV))
    kd = jnp.broadcast_to(kdec.T[:, :, None], (H, C, DK))
    gc = jnp.broadcast_to(gC[:, None, None], (H, 1, DV))
    return dec, cd, kd, gc


def _ret_sample_kernel(st_ref, qt_ref, kt_ref, v_ref, rg_ref, g1_ref, o_ref, ns_ref, *, H):
    qt = qt_ref[0]
    kt = kt_ref[0]
    v = v_ref[0]
    rg = rg_ref[0]
    for h in range(H):
        st = st_ref[0, 0, h]
        qc = qt[:, h:h + 1]
        kc = kt[:, h:h + 1]
        vr = v[h:h + 1, :]
        g1 = g1_ref[h]
        cross = jnp.sum(st * qc, axis=0, keepdims=True)
        qk = jnp.sum(qc * kc, axis=0, keepdims=True)
        o = qk * vr + cross * g1
        ns_ref[0, 0, h] = g1 * st + kc * vr
        ms = jnp.mean(o * o, axis=-1, keepdims=True)
        o_ref[0, h:h + 1, :] = o * lax.rsqrt(ms + EPS) * _silu(rg[h:h + 1, :])


def _retention_sample(state_ret, layer, rq, rk, rv, rg, log_gamma):
    DB = rq.shape[0]
    H, DK, DV = RET_HEADS, RET_DK, RET_DV
    qt = rq.reshape(DB, H, DK).transpose(0, 2, 1)
    kt = rk.reshape(DB, H, DK).transpose(0, 2, 1)
    g1 = jnp.broadcast_to(jnp.exp(1.0 * log_gamma)[:, None, None], (H, 1, DV))
    kern = functools.partial(_ret_sample_kernel, H=H)
    yr, ns = pl.pallas_call(
        kern,
        grid=(DB,),
        in_specs=[
            pl.BlockSpec((1, 1, H, DK, DV), lambda b: (layer, b, 0, 0, 0)),
            pl.BlockSpec((1, DK, H), lambda b: (b, 0, 0)),
            pl.BlockSpec((1, DK, H), lambda b: (b, 0, 0)),
            pl.BlockSpec((1, H, DV), lambda b: (b, 0, 0)),
            pl.BlockSpec((1, H, DV), lambda b: (b, 0, 0)),
            pl.BlockSpec((H, 1, DV), lambda b: (0, 0, 0)),
        ],
        out_specs=[
            pl.BlockSpec((1, H, DV), lambda b: (b, 0, 0)),
            pl.BlockSpec((1, 1, H, DK, DV), lambda b: (0, b, 0, 0, 0)),
        ],
        out_shape=[jax.ShapeDtypeStruct((DB, H, DV), F32),
                   jax.ShapeDtypeStruct((1, DB, H, DK, DV), state_ret.dtype)],
        compiler_params=_cparams("parallel"),
        name="retention_sample",
    )(state_ret, qt, kt, rv.reshape(DB, H, DV), rg.reshape(DB, H, DV), g1)
    return yr.reshape(DB, H * DV), ns[0]


def _sortable_key(x):
    bits = pltpu.bitcast(x, I32)
    return bits ^ ((bits >> 31) & 0x7FFFFFFF)


def _topk_select(keys_ref, nvis, ksel, col_bits, half_ref=None):
    _, R, CW = keys_ref.shape
    ksel_f = float(ksel)

    def lane_fold(f, acc):
        for g in range(CW // LANES):
            acc = acc + f[:, g * LANES:(g + 1) * LANES]
        return acc

    def count(pred):
        body = lambda c, acc: lane_fold(jnp.where(pred(keys_ref[c], c), 1.0, 0.0), acc)
        return jnp.sum(lax.fori_loop(0, nvis, body, jnp.zeros((R, LANES), F32)), axis=1, keepdims=True)

    if half_ref is None:
        def value_bit(bi, u):
            cand_u = u | lax.shift_left(jnp.int32(1), 31 - bi)
            cand = cand_u ^ INT_MIN
            cnt = count(lambda key, c: key >= cand)
            return jnp.where(cnt >= ksel_f, cand_u, u)

        T = lax.fori_loop(0, 32, value_bit, jnp.zeros((R, 1), I32)) ^ INT_MIN
    else:
        I16 = jnp.int16
        HALF = 1 << 15

        def row16(v):
            return jnp.concatenate([jnp.broadcast_to(v, (R, LANES)).astype(I16)] * (CW // LANES), axis=1)

        def count16(pred):
            one, zero = jnp.ones((R, CW), I16), jnp.zeros((R, CW), I16)
            body = lambda c, acc: lane_fold(jnp.where(pred(half_ref[c]), one, zero), acc)
            acc = lax.fori_loop(0, nvis, body, jnp.zeros((R, LANES), I16))
            return jnp.sum(acc.astype(F32), axis=1, keepdims=True)

        def search16(target):
            def bit(bi, u):
                cand_u = u | lax.shift_left(jnp.int32(1), 15 - bi)
                cand = row16(cand_u - HALF)
                cnt = count16(lambda half: half >= cand)
                return jnp.where(cnt >= target, cand_u, u)
            return lax.fori_loop(0, 16, bit, jnp.zeros((R, 1), I32))

        def fill_high(c, carry):
            half_ref[c] = (keys_ref[c] >> 16).astype(I16)
            return carry

        lax.fori_loop(0, nvis, fill_high, 0)
        hi = search16(ksel_f) - HALF
        hi16 = row16(hi)
        above = count16(lambda half: half > hi16)

        def fill_low(c, carry):
            low = ((keys_ref[c] & 0xFFFF) - HALF).astype(I16)
            half_ref[c] = jnp.where(half_ref[c] == hi16, low, jnp.full((R, CW), -HALF, I16))
            return carry

        lax.fori_loop(0, nvis, fill_low, 0)
        T = lax.shift_left(hi, 16) | search16(ksel_f - above)

    need = ksel_f - count(lambda key, c: key > T)
    ties = count(lambda key, c: key == T)

    def col_of(c):
        return c * CW + lax.broadcasted_iota(I32, (R, CW), 1)

    def col_search():
        def col_bit(bi, x):
            cand = x | lax.shift_left(jnp.int32(1), col_bits - 1 - bi)
            cnt = count(lambda key, c: (key == T) & (col_of(c) < cand))
            return jnp.where(cnt < need, cand, x)
        return lax.fori_loop(0, col_bits, col_bit, jnp.zeros((R, 1), I32))

    contested = jnp.max(jnp.where((ties > need) & (T != INT_MIN), 1.0, 0.0)) > 0.0
    J = lax.cond(contested, col_search, lambda: jnp.full((R, 1), 2 ** 30, I32))
    J = jnp.where(T == INT_MIN, -1, J)
    return T, J


def _dsa_select_kernel(iq_ref, iw_ref, ik_ref, bias_ref, keys_ref, half_ref, *, ksel, col_bits):
    NC, QB, CW = keys_ref.shape
    i = pl.program_id(0)
    nvis = ((i + 1) * QB + CW - 1) // CW
    w = iw_ref[...] * (IDX_HEADS ** -0.5 * IDX_DIM ** -0.5)
    rowpos = i * QB + lax.broadcasted_iota(I32, (QB, 1), 0)

    def col_of(c):
        return c * CW + lax.broadcasted_iota(I32, (QB, CW), 1)

    def score_chunk(c, carry):
        kc = ik_ref[c]
        col = c * CW + lax.broadcasted_iota(I32, (Q_BLOCK, CW), 1)
        for r0 in range(0, QB, Q_BLOCK):
            rows = slice(r0, r0 + Q_BLOCK)
            acc = jnp.zeros((Q_BLOCK, CW), F32)
            for h in range(IDX_HEADS):
                s = _dot_nt(iq_ref[rows, h * IDX_DIM:(h + 1) * IDX_DIM], kc)
                acc = acc + jnp.maximum(s, 0.0) * w[rows, h:h + 1]
            keys_ref[c, rows, :] = jnp.where(col <= rowpos[rows], _sortable_key(acc), INT_MIN)
        return carry

    lax.fori_loop(0, nvis, score_chunk, 0)
    T, J = _topk_select(keys_ref, nvis, ksel, col_bits, half_ref)

    def emit(c, carry):
        key = keys_ref[c]
        col = col_of(c)
        sel = ((key > T) | ((key == T) & (col <= J))) & (col <= rowpos)
        bias_ref[0, c] = jnp.where(sel, 0.0, NEG_BIAS).astype(bias_ref.dtype)
        return carry

    lax.fori_loop(0, nvis, emit, 0)

    def fill(c, carry):
        bias_ref[0, c] = jnp.full((QB, CW), NEG_BIAS, bias_ref.dtype)
        return carry

    lax.fori_loop(nvis, NC, fill, 0)


def _dsa_select_prompt(iq, iw, ik, ksel, QB, CW):
    S = iq.shape[0]
    NQ, NC = S // QB, S // CW
    kern = functools.partial(_dsa_select_kernel, ksel=ksel, col_bits=int(np.ceil(np.log2(S))))
    return pl.pallas_call(
        kern,
        grid=(NQ,),
        in_specs=[
            pl.BlockSpec((QB, iq.shape[1]), lambda i: (i, 0)),
            pl.BlockSpec((QB, iw.shape[1]), lambda i: (i, 0)),
            pl.BlockSpec((NC, CW, ik.shape[1]), lambda i: (0, 0, 0)),
        ],
        out_specs=pl.BlockSpec((1, NC, QB, CW), lambda i: (i, 0, 0, 0)),
        out_shape=jax.ShapeDtypeStruct((NQ, NC, QB, CW), BF16),
        scratch_shapes=[pltpu.VMEM((NC, QB, CW), I32), pltpu.VMEM((NC, QB, CW), jnp.int16)],
        compiler_params=_cparams("parallel"),
        name="dsa_select_prompt",
    )(iq, iw, ik.reshape(NC, CW, ik.shape[1]))


def _dsa_attn_kernel(qi_ref, kj_ref, fin_ref, q_ref, k_ref, v_ref, b_ref, o_ref, m_sc, l_sc, acc_sc, s_sc, a_sc,
                     *, H, DH, TK, SUB):
    step = pl.program_id(0)

    @pl.when(kj_ref[step] == 0)
    def _():
        m_sc[...] = jnp.full(m_sc.shape, NEG_BIAS, F32)
        l_sc[...] = jnp.zeros_like(l_sc)
        acc_sc[...] = jnp.zeros_like(acc_sc)

    ones = jnp.ones((SUB, LANES), BF16)
    for t in range(TK // SUB):
        keys = slice(t * SUB, (t + 1) * SUB)
        for h in range(H):
            cols = slice(h * DH, (h + 1) * DH)
            s_sc[h] = _dot_nt(q_ref[:, cols], k_ref[keys, cols]).astype(BF16) + b_ref[0, 0, :, keys]
        for h in range(H):
            s = s_sc[h]
            m_prev = m_sc[h]
            m_new = jnp.maximum(m_prev, jnp.max(s, axis=-1, keepdims=True).astype(F32))
            a_sc[h] = jnp.exp2(m_prev - m_new)
            s_sc[h] = jnp.exp2(s - jnp.concatenate([m_new.astype(BF16)] * (SUB // LANES), axis=1))
            m_sc[h] = m_new
        for h in range(H):
            cols = slice(h * DH, (h + 1) * DH)
            pv = _dot(s_sc[h], jnp.concatenate([v_ref[keys, cols], ones], axis=1))
            alpha = a_sc[h]
            l_sc[h] = alpha * l_sc[h] + pv[:, DH:]
            acc_sc[:, cols] = alpha * acc_sc[:, cols] + pv[:, :DH]

    @pl.when(fin_ref[step] == 1)
    def _():
        for h in range(H):
            cols = slice(h * DH, (h + 1) * DH)
            o_ref[:, cols] = (acc_sc[:, cols] / l_sc[h]).astype(o_ref.dtype)


def _dsa_attn_prompt(aq, ak, av, bias, QB, TK):
    S, W = aq.shape
    H, DH = ATT_HEADS, ATT_DH
    assert DH == LANES
    bq = bias.shape[2] // QB
    pairs = [(i, j) for i in range(S // QB) for j in range(((i + 1) * QB - 1) // TK + 1)]
    qi = jnp.asarray([p[0] for p in pairs], I32)
    kj = jnp.asarray([p[1] for p in pairs], I32)
    fin = jnp.asarray([int(p[1] == ((p[0] + 1) * QB - 1) // TK) for p in pairs], I32)
    kern = functools.partial(_dsa_attn_kernel, H=H, DH=DH, TK=TK, SUB=_pick(TK, (512, 256, 128)))
    return pl.pallas_call(
        kern,
        grid_spec=pltpu.PrefetchScalarGridSpec(
            num_scalar_prefetch=3,
            grid=(len(pairs),),
            in_specs=[
                pl.BlockSpec((QB, W), lambda s, qi, kj, fin: (qi[s], 0)),
                pl.BlockSpec((TK, W), lambda s, qi, kj, fin: (kj[s], 0)),
                pl.BlockSpec((TK, W), lambda s, qi, kj, fin: (kj[s], 0)),
                pl.BlockSpec((1, 1, QB, TK), lambda s, qi, kj, fin: (qi[s] // bq, kj[s], qi[s] % bq, 0)),
            ],
            out_specs=pl.BlockSpec((QB, W), lambda s, qi, kj, fin: (qi[s], 0)),
            scratch_shapes=[pltpu.VMEM((H, QB, LANES), F32), pltpu.VMEM((H, QB, LANES), F32),
                            pltpu.VMEM((QB, W), F32), pltpu.VMEM((H, QB, _pick(TK, (512, 256, 128))), BF16),
                            pltpu.VMEM((H, QB, LANES), F32)],
        ),
        out_shape=jax.ShapeDtypeStruct((S, W), BF16),
        compiler_params=_cparams("arbitrary"),
        name="dsa_attn_prompt",
    )(qi, kj, fin, aq, ak, av, bias)


def _idx_sample_kernel(pt_ref, iq_ref, w_ref, *refs):
    page_refs, o_ref = refs[:-1], refs[-1]
    iq = iq_ref[0].astype(BF16)
    w = w_ref[0]
    rows = []
    for page_ref in page_refs:
        s = _dot_nt(iq, page_ref[0, 0].astype(BF16))
        rows.append(jnp.sum(jnp.maximum(s, 0.0) * w, axis=0, keepdims=True))
    o_ref[0, 0] = jnp.concatenate(rows, axis=1)


def _idx_sample(cache_idx_k, layer, page_table, iq, iw):
    DB, NP = page_table.shape
    P, Di = cache_idx_k.shape[2:]
    Hi = IDX_HEADS
    G = _pick(NP, (32, 16, 8, 4, 2, 1))
    w = jnp.broadcast_to((iw * (IDX_HEADS ** -0.5 * IDX_DIM ** -0.5))[:, :, None], (DB, Hi, P))

    def page_spec(r):
        return pl.BlockSpec((1, 1, P, Di), lambda b, g, pt: (layer, pt[b, g * G + r], 0, 0))

    out = pl.pallas_call(
        _idx_sample_kernel,
        grid_spec=pltpu.PrefetchScalarGridSpec(
            num_scalar_prefetch=1,
            grid=(DB, NP // G),
            in_specs=[
                pl.BlockSpec((1, Hi, Di), lambda b, g, pt: (b, 0, 0)),
                pl.BlockSpec((1, Hi, P), lambda b, g, pt: (b, 0, 0)),
            ] + [page_spec(r) for r in range(G)],
            out_specs=pl.BlockSpec((1, 1, 1, G * P), lambda b, g, pt: (b, g, 0, 0)),
        ),
        out_shape=jax.ShapeDtypeStruct((DB, NP // G, 1, G * P), F32),
        compiler_params=_cparams("parallel", "arbitrary"),
        name="dsa_index_sample",
    )(page_table, iq.reshape(DB, Hi, Di), w, *([cache_idx_k] * G))
    return out.reshape(DB, NP * P)


def _select_sample_kernel(sc_ref, iq_ref, ikn_ref, iw_ref, idx_ref, keys_ref, cnt_ref, *, ksel, past_len,
                          col_bits):
    NC, R, CW = keys_ref.shape
    NP = NC - 1
    w = iw_ref[...] * (IDX_HEADS ** -0.5 * IDX_DIM ** -0.5)
    ikn = ikn_ref[...]
    new = jnp.zeros((R, 1), F32)
    for h in range(IDX_HEADS):
        s = jnp.sum(iq_ref[:, h * IDX_DIM:(h + 1) * IDX_DIM] * ikn, axis=-1, keepdims=True)
        new = new + jnp.maximum(s, 0.0) * w[:, h:h + 1]

    def col_of(c):
        return c * CW + lax.broadcasted_iota(I32, (R, CW), 1)

    def load(c, carry):
        keys_ref[c] = _sortable_key(sc_ref[c] + 0.0)
        return carry

    lax.fori_loop(0, NP, load, 0)
    keys_ref[NP] = jnp.where(col_of(NP) <= past_len, _sortable_key(jnp.broadcast_to(new, (R, CW))), INT_MIN)
    T, J = _topk_select(keys_ref, NC, ksel, col_bits)

    tri = (lax.broadcasted_iota(I32, (CW, CW), 0) <= lax.broadcasted_iota(I32, (CW, CW), 1)).astype(BF16)

    def prefix(c, before):
        key = keys_ref[c]
        col = col_of(c)
        sel = ((key > T) | ((key == T) & (col <= J))) & (col <= past_len)
        f = jnp.where(sel, 1.0, 0.0)
        cnt_ref[c] = _dot(f.astype(BF16), tri) + before
        return before + jnp.sum(f, axis=-1, keepdims=True)

    lax.fori_loop(0, NC, prefix, jnp.zeros((R, 1), F32))

    jcol = lax.broadcasted_iota(I32, (ksel, 1), 0).astype(F32)
    lane = lax.broadcasted_iota(I32, (ksel, LANES), 1)

    def one_row(r, out):
        def chunk(c, acc):
            return acc + jnp.where(cnt_ref[c, pl.ds(r, 1), :] <= jcol, 1.0, 0.0)
        acc = lax.fori_loop(0, NC, chunk, jnp.zeros((ksel, CW), F32))
        return jnp.where(lane == r, jnp.sum(acc, axis=-1, keepdims=True), out)

    idx_ref[...] = lax.fori_loop(0, R, one_row, jnp.zeros((ksel, LANES), F32)).astype(I32)


def _select_sample(scores, iq, ik_new, iw, ksel):
    DB = scores.shape[0]
    P = LANES
    NP = scores.shape[1] // P
    past_len = NP * P
    assert DB <= LANES
    sc = scores.reshape(DB, NP, P).transpose(1, 0, 2)
    kern = functools.partial(_select_sample_kernel, ksel=ksel, past_len=past_len,
                             col_bits=int(np.ceil(np.log2(past_len + P))))
    idx_t = pl.pallas_call(
        kern,
        out_shape=jax.ShapeDtypeStruct((ksel, LANES), I32),
        scratch_shapes=[pltpu.VMEM((NP + 1, DB, P), I32), pltpu.VMEM((NP + 1, DB, P), F32)],
        compiler_params=pltpu.CompilerParams(vmem_limit_bytes=VMEM_LIMIT_BYTES),
        name="dsa_select_sample",
    )(sc, iq, ik_new, iw)
    return idx_t[:, :DB].T


def _attn_sample_kernel(idx_ref, pt_ref, q_ref, kn_ref, vn_ref, ck_hbm, cv_hbm, o_ref, kbuf, vbuf, sem,
                        m_sc, l_sc, acc_sc, *, layer, G, ksel, past_len, page):
    b = pl.program_id(0)
    slot = b % 2
    scale = q_ref.shape[-1] ** -0.5

    def row_copies(seq, r, sl):
        pos = jnp.minimum(idx_ref[seq * ksel + r], past_len - 1)
        src = (layer, pt_ref[seq, pos // page], pos % page)
        return (pltpu.make_async_copy(ck_hbm.at[src], kbuf.at[sl, r], sem.at[sl, 0]),
                pltpu.make_async_copy(cv_hbm.at[src], vbuf.at[sl, r], sem.at[sl, 1]))

    def start_rows(seq, sl):
        def body(r, carry):
            for cp in row_copies(seq, r, sl):
                cp.start()
            return carry
        lax.fori_loop(0, ksel, body, 0)

    @pl.when(b == 0)
    def _():
        start_rows(0, 0)

    @pl.when(b + 1 < pl.num_programs(0))
    def _():
        start_rows(b + 1, 1 - slot)

    def wait_row(r, carry):
        for cp in row_copies(b, r, slot):
            cp.wait()
        return carry

    lax.fori_loop(0, ksel, wait_row, 0)

    m_sc[...] = jnp.full(m_sc.shape, NEG_BIAS, F32)
    l_sc[...] = jnp.zeros_like(l_sc)
    acc_sc[...] = jnp.zeros_like(acc_sc)
    q = q_ref[0]

    def slot_bias(r):
        pos = idx_ref[b * ksel + r]
        return jnp.full((1, 1, 1), jnp.where(pos >= past_len, NEG_BIAS, 0.0), F32)

    def update(kk, vv, bias3):
        s = jnp.sum(kk * q[None], axis=-1, keepdims=True) * scale + bias3
        m_prev = m_sc[...]
        m_new = jnp.maximum(m_prev, jnp.max(s, axis=0))
        alpha = jnp.exp(m_prev - m_new)
        pr = jnp.exp(s - m_new[None])
        l_sc[...] = alpha * l_sc[...] + jnp.sum(pr, axis=0)
        acc_sc[...] = alpha * acc_sc[...] + jnp.sum(pr * vv, axis=0)
        m_sc[...] = m_new

    for c in range(ksel // G):
        rows = pl.ds(c * G, G)
        update(kbuf[slot, rows], vbuf[slot, rows],
               jnp.concatenate([slot_bias(c * G + r) for r in range(G)], axis=0))

    last = idx_ref[b * ksel + ksel - 1]
    update(kn_ref[...], vn_ref[...], jnp.full((1, 1, 1), jnp.where(last >= past_len, 0.0, NEG_BIAS), F32))
    o_ref[0] = acc_sc[...] / l_sc[...]


def _attn_sample(cache_k, cache_v, layer, page_table, aq, ak_new, av_new, idx):
    DB, NP = page_table.shape
    P, H, DH = cache_k.shape[2:]
    ksel = idx.shape[1]
    vec_spec = pl.BlockSpec((1, H, DH), lambda b, idx_ref, pt: (b, 0, 0))
    kern = functools.partial(_attn_sample_kernel, layer=layer, G=_pick(ksel, (32, 16, 8, 4, 2, 1)), ksel=ksel,
                             past_len=NP * P, page=P)
    out = pl.pallas_call(
        kern,
        grid_spec=pltpu.PrefetchScalarGridSpec(
            num_scalar_prefetch=2,
            grid=(DB,),
            in_specs=[vec_spec, vec_spec, vec_spec, pl.BlockSpec(memory_space=pl.ANY),
                      pl.BlockSpec(memory_space=pl.ANY)],
            out_specs=vec_spec,
            scratch_shapes=[pltpu.VMEM((2, ksel, H, DH), F32), pltpu.VMEM((2, ksel, H, DH), F32),
                            pltpu.SemaphoreType.DMA((2, 2)),
                            pltpu.VMEM((H, 1), F32), pltpu.VMEM((H, 1), F32), pltpu.VMEM((H, DH), F32)],
        ),
        out_shape=jax.ShapeDtypeStruct((DB, H, DH), F32),
        compiler_params=_cparams("arbitrary"),
        name="dsa_attn_sample",
    )(idx.reshape(-1), page_table, aq.reshape(DB, H, DH), ak_new.reshape(DB, H, DH), av_new.reshape(DB, H, DH),
      cache_k, cache_v)
    return out.reshape(DB, H * DH)


def _rotary_tables(pos):
    d = RET_DK
    inv_freq = 1.0 / (ROPE_BASE ** jnp.linspace(0.0, 1.0, d // 2, dtype=F32))
    ang = pos.astype(F32)[:, None] * inv_freq[None, :]
    cos = jnp.repeat(jnp.cos(ang), 2, axis=1)
    sin = jnp.sin(ang)
    sin_signed = jnp.stack([-sin, sin], axis=-1).reshape(pos.shape[0], d)
    return cos, sin_signed


def _in_proj_columns(w_in):
    rqk, rv_w = RET_HEADS * RET_DK, RET_HEADS * RET_DV
    aw, iqw = ATT_HEADS * ATT_DH, IDX_HEADS * IDX_DIM
    d_model = w_in.shape[1]
    names = ("rq", "rk", "rv", "rg", "aq", "ak", "av", "iq", "ik", "iw", "ga", "gb")
    sizes = (rqk, rqk, rv_w, rv_w, aw, aw, aw, iqw, IDX_DIM, IDX_HEADS, d_model, d_model)
    offs = np.concatenate([[0], np.cumsum(sizes)]).tolist()
    assert offs[-1] == w_in.shape[2]
    return {n: (offs[t], sizes[t]) for t, n in enumerate(names)}


def _mix_inputs(hp, hs, w_in_t, layer, cols, qg, kg, rot_p, rot_s, q_scale):
    tm = _pick(hp.shape[0], ROW_TILES)
    P, Sm = {}, {}

    def proj(name, epi_p, ex_p, out_p, epi_s, ex_s, out_s, tn_prefs=(512, 256, 128), n=None):
        col0 = cols[name][0]
        n = n or cols[name][1]
        outs = _mm([hp], [(0, w_in_t, layer, col0, n, True)], ex_p, epi_p, [d for _, d in out_p], tm=tm,
                   tn=_pick(n, tn_prefs), name="proj_" + name, rider=([hs], ex_s, epi_s, [d for _, d in out_s]))
        for (key, _), o in zip(out_p, outs):
            P[key] = o
        for (key, _), o in zip(out_s, outs[len(out_p):]):
            Sm[key] = o

    def same(name, dt_p, **kw):
        proj(name, _epi_identity, [], [(name, dt_p)], _epi_identity, [], [(name, F32)], **kw)

    rot_p = [(t, "rows") for t in rot_p]
    rot_s = [(t, "rows") for t in rot_s]
    for name, scale in (("rq", 1.0), ("rk", RET_DK ** -0.5)):
        epi = functools.partial(_epi_rotary, scale)
        proj(name, epi, rot_p, [(name, BF16)], epi, rot_s, [(name, F32)])
    same("rv", BF16)
    same("rg", F32)
    norm1, norm2 = functools.partial(_epi_head_norm, 1), functools.partial(_epi_head_norm, 2)
    proj("aq", norm1, [(qg.reshape(1, -1) * q_scale, "full")], [("aq", BF16)],
         norm1, [(qg.reshape(1, -1), "full")], [("aq", F32)])
    kgain = [(kg.reshape(1, -1), "full")]
    proj("ak", norm2, kgain, [("ak", F32), ("ak16", BF16)], norm1, kgain, [("ak", F32)])
    proj("av", _epi_two_copies, [], [("av", F32), ("av16", BF16)], _epi_identity, [], [("av", F32)])
    same("iq", BF16)
    same("ik", F32, tn_prefs=(2 * LANES,), n=2 * LANES)
    same("ga", F32)
    same("gb", F32)
    for d in (P, Sm):
        ikw = d["ik"]
        d["ik"] = ikw[:, :IDX_DIM]
        d["iw"] = ikw[:, IDX_DIM:IDX_DIM + IDX_HEADS]
    return P, Sm


def _merge_ffn(xp, xs, mp, ms, layer, w_ret_o, w_att_o, w_out, g_ffn, w_gate, w_up, wd):
    M, D = xp.shape
    F = w_gate.shape[2]
    tm = _pick(M, ROW_TILES)
    gates = lambda t: [(t[2], "tile"), (t[3], "tile")]
    m_p, m_s = _mm(list(mp[:2]), [(0, w_ret_o, layer, 0, D, False), (1, w_att_o, layer, 0, D, False)], gates(mp),
                   _epi_merge, [BF16], tm=_pick(M, ROW_TILES[1:]), tn=_pick(D, (512, 256, 128)), name="merge",
                   rider=(list(ms[:2]), gates(ms), _epi_merge, [BF16]))
    x1p, x1s = _mm([m_p], [(0, w_out, layer, 0, D, False)], [(xp, "tile")], _epi_residual, [F32], tm=tm,
                   tn=_pick(D, (512, 256, 128)), name="out_proj",
                   rider=([m_s], [(xs, "tile")], _epi_residual, [F32]))
    hfp, hfs = _rmsnorm(x1p, g_ffn), _rmsnorm(x1s, g_ffn)
    up, us = _mm([hfp], [(0, w_gate, layer, 0, F, False), (0, w_up, layer, 0, F, False)], [], _epi_swiglu, [BF16],
                 tm=tm, tn=_pick(F, (256, 128)), name="ffn_up", rider=([hfs], [], _epi_swiglu, [BF16]))
    down = lambda u, x1: _mm([u], [(0, wd, None, 0, D, False)], [(x1, "tile")], _epi_residual, [F32],
                             tm=_pick(u.shape[0], ROW_TILES[1:]), tn=_pick(D, (256, 128)), name="ffn_down",
                             hold_b=False)[0]
    return down(up, x1p), down(us, x1s)


def kernel(x_prompt, x_sample, cache_k, cache_v, cache_idx_k, state_ret, page_table, norm_mix_g, w_in,
           q_norm_g, k_norm_g, w_ret_o, w_att_o, w_out, norm_ffn_g, w_ffn_gate, w_ffn_up, w_ffn_down):
    depth = w_in.shape[0]
    B, S, D = x_prompt.shape
    DB, T, _ = x_sample.shape
    assert B == 1 and T == 1
    page_size = cache_k.shape[2]
    past_len = page_table.shape[1] * page_size

    log_gamma = jnp.log1p(-jnp.exp2(-5.0 - jnp.arange(RET_HEADS, dtype=F32)))
    ret_tabs = _retention_tables(log_gamma)
    cos_p, sin_p = _rotary_tables(jnp.arange(S, dtype=I32))
    cos_s, sin_s = _rotary_tables(jnp.full((DB,), past_len, I32))
    ksel_p = min(TOPK_MAX, S // 4)
    ksel_s = min(TOPK_MAX, (past_len + T) // 4)
    cw = _pick(S, (512, 256, 128))
    qb = _pick(S, (2 * Q_BLOCK, Q_BLOCK))
    cols = _in_proj_columns(w_in)
    w_in_t = jnp.swapaxes(w_in, 1, 2)

    xp = x_prompt.reshape(S, D)
    xs = x_sample.reshape(DB, D)
    kp, vp, ikp, stp, ksm, vsm, iks, sts = [], [], [], [], [], [], [], []
    for layer in range(depth):
        wd = w_ffn_down[layer].astype(BF16)
        dense = (layer, w_ret_o, w_att_o, w_out, norm_ffn_g[layer], w_ffn_gate, w_ffn_up, wd)

        p, s = _mix_inputs(_rmsnorm(xp, norm_mix_g[layer]), _rmsnorm(xs, norm_mix_g[layer]), w_in_t, layer, cols,
                           q_norm_g[layer], k_norm_g[layer], (cos_p, sin_p), (cos_s, sin_s),
                           ATT_DH ** -0.5 * float(np.log2(np.e)))

        yr_p, st_p = _retention_prompt(p["rq"], p["rk"], p["rv"], p["rg"], ret_tabs)
        bias = _dsa_select_prompt(p["iq"], p["iw"], p["ik"].astype(BF16), ksel_p, qb, cw)
        ya_p = _dsa_attn_prompt(p["aq"], p["ak16"], p["av16"], bias, qb, cw)

        yr_s, st_s = _retention_sample(state_ret, layer, s["rq"], s["rk"], s["rv"], s["rg"], log_gamma)
        scores = _idx_sample(cache_idx_k, layer, page_table, s["iq"], s["iw"])
        sel_idx = _select_sample(scores, s["iq"], s["ik"], s["iw"], ksel_s)
        ya_s = _attn_sample(cache_k, cache_v, layer, page_table, s["aq"], s["ak"], s["av"], sel_idx)

        xp, xs = _merge_ffn(xp, xs, (yr_p, ya_p, p["ga"], p["gb"]),
                            (yr_s.astype(BF16), ya_s.astype(BF16), s["ga"], s["gb"]), *dense)
        kp.append(p["ak"].reshape(B, S, ATT_HEADS, ATT_DH))
        vp.append(p["av"].reshape(B, S, ATT_HEADS, ATT_DH))
        ikp.append(p["ik"].reshape(B, S, IDX_DIM))
        stp.append(st_p[None].astype(state_ret.dtype))
        ksm.append(s["ak"].reshape(DB, T, ATT_HEADS, ATT_DH))
        vsm.append(s["av"].reshape(DB, T, ATT_HEADS, ATT_DH))
        iks.append(s["ik"].reshape(DB, T, IDX_DIM))
        sts.append(st_s)
    return (xp.reshape(B, S, D), xs.reshape(DB, T, D), jnp.stack(kp), jnp.stack(vp), jnp.stack(ikp),
            jnp.stack(stp), jnp.stack(ksm), jnp.stack(vsm), jnp.stack(iks), jnp.stack(sts))
```

```python
import functools

import numpy as np
import jax
import jax.numpy as jnp
from jax import lax
from jax.experimental import pallas as pl
from jax.experimental.pallas import tpu as pltpu

F32 = jnp.float32
BF16 = jnp.bfloat16
I32 = jnp.int32

RET_HEADS = 8
RET_DK = 256
RET_DV = 512
RET_CHUNK = 128
ROPE_BASE = 10000.0
ATT_HEADS = 16
ATT_DH = 128
IDX_HEADS = 8
IDX_DIM = 128
TOPK_MAX = 256
Q_BLOCK = 128
EPS = 1e-6

LANES = 128
VMEM_LIMIT_BYTES = 56 * 1024 * 1024

ROW_TILES = (1024, 512, 256, 128, 64, 32, 16, 8)

NEG_BIAS = -1e30
INT_MIN = -(2 ** 31)


def _cparams(*sem):
    return pltpu.CompilerParams(dimension_semantics=sem, vmem_limit_bytes=VMEM_LIMIT_BYTES)


def _pick(n, prefs):
    for p in prefs:
        if p <= n and n % p == 0:
            return p
    return n


def _dot(a, b):
    return jnp.dot(a, b, preferred_element_type=F32)


def _dot_nt(a, b):
    return lax.dot_general(a, b, (((1,), (1,)), ((), ())), preferred_element_type=F32)


def _sigmoid(x):
    return 1.0 / (1.0 + jnp.exp(-x))


def _silu(x):
    return x * _sigmoid(x)


def _rmsnorm_kernel(x_ref, g_ref, o_ref):
    x = x_ref[...]
    ms = jnp.mean(x * x, axis=-1, keepdims=True)
    o_ref[...] = (x * lax.rsqrt(ms + EPS) * g_ref[...]).astype(o_ref.dtype)


def _rmsnorm(x, g):
    M, D = x.shape
    tm = _pick(M, (256, 128, 64, 32, 16, 8))
    return pl.pallas_call(
        _rmsnorm_kernel,
        grid=(M // tm,),
        in_specs=[pl.BlockSpec((tm, D), lambda i: (i, 0)), pl.BlockSpec((1, D), lambda i: (0, 0))],
        out_specs=pl.BlockSpec((tm, D), lambda i: (i, 0)),
        out_shape=jax.ShapeDtypeStruct((M, D), BF16),
        compiler_params=_cparams("parallel"),
        name="rmsnorm",
    )(x, g.reshape(1, D))


def _mm_kernel(*refs, n_a, b_src, b_nt, n_e, o_kinds, cast, epilogue, rider):
    n_o = len(o_kinds)
    n_b = len(b_src)
    take = lambda n, it=iter(refs): [next(it) for _ in range(n)]
    a_refs, b_refs, e_refs = take(n_a), take(n_b), take(n_e)
    if rider:
        n_e2, n_o2, epilogue2 = rider
        a2_refs, e2_refs = take(n_a), take(n_e2)
    o_refs = take(n_o)
    if rider:
        o2_refs = take(n_o2)
    if cast:
        w_refs = take(n_b)

    def compute(a_rs, e_rs, o_rs, epi, kinds):
        bs = w_refs if cast else b_refs
        accs = [(_dot_nt if nt else _dot)(a_rs[s][...], b[...]) for s, nt, b in zip(b_src, b_nt, bs)]
        for o, r, kind in zip(o_rs, epi(accs, [e[...] for e in e_rs]), kinds):
            if kind == "slabs":
                for rg in range(o.shape[0]):
                    for cg in range(o.shape[1]):
                        o[rg, cg] = r[rg * LANES:(rg + 1) * LANES, cg * LANES:(cg + 1) * LANES].astype(o.dtype)
            elif kind == "T":
                o[...] = r.T.astype(o.dtype)
            else:
                o[...] = r.astype(o.dtype)

    if cast or rider:
        @pl.when(pl.program_id(1) == 0)
        def _():
            if cast:
                for b, w in zip(b_refs, w_refs):
                    w[...] = b[...].reshape(w.shape).astype(w.dtype)
            if rider:
                compute(a2_refs, e2_refs, o2_refs, epilogue2, ("plain",) * n_o2)
    compute(a_refs, e_refs, o_refs, epilogue, o_kinds)


def _mm(a_list, b_list, extras, epilogue, out_dtypes, *, tm, tn, name, hold_b=True, rider=None):
    M = a_list[0].shape[0]
    N = b_list[0][4]
    cast = b_list[0][2] is not None
    assert all((b[2] is not None) == cast for b in b_list) and (hold_b or not (cast or rider))
    if hold_b:
        grid = (N // tn, M // tm)
        ij = lambda f: (lambda j, i: f(i, j))
    else:
        grid = (M // tm, N // tn)
        ij = lambda f: f
    in_specs, args, scratch = [], [], []
    for a in a_list:
        in_specs.append(pl.BlockSpec((tm, a.shape[1]), ij(lambda i, j: (i, 0))))
        args.append(a)
    for _, w, layer, col0, _, nt in b_list:
        if nt:
            K = w.shape[2]
            if col0 % tn == 0:
                spec = pl.BlockSpec((None, tn, K), ij(lambda i, j, l=layer, c=col0 // tn: (l, c + j, 0)))
            else:
                spec = pl.BlockSpec((pl.Element(1), pl.Element(tn), pl.Element(K)),
                                    ij(lambda i, j, l=layer, c=col0: (l, pl.multiple_of(c + j * tn, 8), 0)))
            scratch.append(pltpu.VMEM((tn, K), BF16))
        elif cast:
            assert col0 % tn == 0
            K = w.shape[1]
            spec = pl.BlockSpec((None, K, tn), ij(lambda i, j, l=layer, c=col0 // tn: (l, 0, c + j)))
            scratch.append(pltpu.VMEM((K, tn), BF16))
        else:
            assert col0 % tn == 0
            spec = pl.BlockSpec((w.shape[0], tn), ij(lambda i, j, c=col0 // tn: (0, c + j)))
        in_specs.append(spec)
        args.append(w)
    def add_extras(ex_list, rows, row_index):
        for arr, kind in ex_list:
            if kind == "tile":
                in_specs.append(pl.BlockSpec((rows, tn), ij(lambda i, j: (row_index(i), j))))
            elif kind == "rows":
                in_specs.append(pl.BlockSpec((rows, arr.shape[1]), ij(lambda i, j: (row_index(i), 0))))
            else:
                in_specs.append(pl.BlockSpec(arr.shape, ij(lambda i, j, nd=arr.ndim: (0,) * nd)))
            args.append(arr)

    add_extras(extras, tm, lambda i: i)
    out_specs, out_shape, o_kinds = [], [], []
    for od in out_dtypes:
        dt, kind = od if isinstance(od, tuple) else (od, "plain")
        o_kinds.append(kind)
        if kind == "T":
            out_specs.append(pl.BlockSpec((tn, tm), ij(lambda i, j: (j, i))))
            out_shape.append(jax.ShapeDtypeStruct((N, M), dt))
        elif kind == "slabs":
            out_specs.append(pl.BlockSpec((tm // LANES, tn // LANES, LANES, LANES), ij(lambda i, j: (i, j, 0, 0))))
            out_shape.append(jax.ShapeDtypeStruct((M // LANES, N // LANES, LANES, LANES), dt))
        else:
            out_specs.append(pl.BlockSpec((tm, tn), ij(lambda i, j: (i, j))))
            out_shape.append(jax.ShapeDtypeStruct((M, N), dt))
    if rider:
        a_list2, extras2, epilogue2, out_dtypes2 = rider
        M2 = a_list2[0].shape[0]
        for a in a_list2:
            in_specs.append(pl.BlockSpec((M2, a.shape[1]), ij(lambda i, j: (0, 0))))
            args.append(a)
        add_extras(extras2, M2, lambda i: 0)
        out_specs += [pl.BlockSpec((M2, tn), ij(lambda i, j: (0, j))) for _ in out_dtypes2]
        out_shape += [jax.ShapeDtypeStruct((M2, N), od) for od in out_dtypes2]
        rider = (len(extras2), len(out_dtypes2), epilogue2)
    kern = functools.partial(_mm_kernel, n_a=len(a_list), b_src=tuple(b[0] for b in b_list),
                             b_nt=tuple(b[5] for b in b_list), n_e=len(extras), o_kinds=tuple(o_kinds), cast=cast,
                             epilogue=epilogue, rider=rider)
    outs = pl.pallas_call(
        kern,
        grid=grid,
        in_specs=in_specs,
        out_specs=out_specs,
        out_shape=out_shape,
        scratch_shapes=scratch,
        compiler_params=_cparams("parallel", "arbitrary"),
        name=name,
    )(*args)
    return outs


def _epi_identity(accs, extras):
    return [accs[0]]


def _epi_two_copies(accs, extras):
    return [accs[0], accs[0]]


def _epi_rotary(scale, accs, extras):
    acc = accs[0]
    cos, sin_signed = extras
    dk = cos.shape[1]
    lane = lax.broadcasted_iota(I32, (acc.shape[0], dk), 1)
    even = (lane % 2) == 0
    outs = []
    for h in range(acc.shape[1] // dk):
        x = acc[:, h * dk:(h + 1) * dk]
        swapped = jnp.where(even, pltpu.roll(x, dk - 1, 1), pltpu.roll(x, 1, 1))
        outs.append((x * cos + swapped * sin_signed) * scale)
    return [jnp.concatenate(outs, axis=1) if len(outs) > 1 else outs[0]]


def _epi_head_norm(n_out, accs, extras):
    acc = accs[0]
    g = extras[0]
    dh = g.shape[1]
    outs = []
    for h in range(acc.shape[1] // dh):
        x = acc[:, h * dh:(h + 1) * dh]
        ms = jnp.mean(x * x, axis=-1, keepdims=True)
        outs.append(x * lax.rsqrt(ms + EPS) * g)
    y = jnp.concatenate(outs, axis=1) if len(outs) > 1 else outs[0]
    return [y] * n_out


def _epi_merge(accs, extras):
    ga, gb = extras
    return [_sigmoid(ga) * accs[0] + _sigmoid(gb) * accs[1]]


def _epi_residual(accs, extras):
    return [extras[0] + accs[0]]


def _epi_swiglu(accs, extras):
    return [_silu(accs[0]) * accs[1]]


def _ret_prompt_kernel(q_ref, k_ref, v_ref, rg_ref, dec_ref, cd_ref, kd_ref, gc_ref, yr_ref, st_ref,
                       *, nsub, C):
    @pl.when(pl.program_id(1) == 0)
    def _():
        st_ref[...] = jnp.zeros_like(st_ref)

    DK, DV = kd_ref.shape[2], cd_ref.shape[2]
    for t in range(nsub):
        rows = slice(t * C, (t + 1) * C)
        for g in range(st_ref.shape[0]):
            qk = slice(g * DK, (g + 1) * DK)
            vc = slice(g * DV, (g + 1) * DV)
            q = q_ref[rows, qk]
            k = k_ref[rows, qk]
            v = v_ref[rows, vc]
            st = st_ref[g]
            s = _dot_nt(q, k) * dec_ref[g]
            o = _dot(s.astype(BF16), v) + _dot(q, st.astype(BF16)) * cd_ref[g]
            kdt = (k.astype(F32) * kd_ref[g]).T.astype(BF16)
            st_ref[g] = gc_ref[g] * st + _dot(kdt, v)
            ms = jnp.mean(o * o, axis=-1, keepdims=True)
            yr_ref[rows, vc] = (o * lax.rsqrt(ms + EPS) * _silu(rg_ref[rows, vc])).astype(yr_ref.dtype)


def _retention_prompt(rq, rk, rv, rg, tabs):
    S = rq.shape[0]
    H, DK, DV, C = RET_HEADS, RET_DK, RET_DV, RET_CHUNK
    nsub = _pick(S // C, (4, 2, 1))
    T = nsub * C
    HB = _pick(H, (4, 2, 1))
    dec, cd, kd, gc = tabs
    kern = functools.partial(_ret_prompt_kernel, nsub=nsub, C=C)
    return pl.pallas_call(
        kern,
        grid=(H // HB, S // T),
        in_specs=[
            pl.BlockSpec((T, HB * DK), lambda h, c: (c, h)),
            pl.BlockSpec((T, HB * DK), lambda h, c: (c, h)),
            pl.BlockSpec((T, HB * DV), lambda h, c: (c, h)),
            pl.BlockSpec((T, HB * DV), lambda h, c: (c, h)),
            pl.BlockSpec((HB, C, C), lambda h, c: (h, 0, 0)),
            pl.BlockSpec((HB, C, DV), lambda h, c: (h, 0, 0)),
            pl.BlockSpec((HB, C, DK), lambda h, c: (h, 0, 0)),
            pl.BlockSpec((HB, 1, DV), lambda h, c: (h, 0, 0)),
        ],
        out_specs=[
            pl.BlockSpec((T, HB * DV), lambda h, c: (c, h)),
            pl.BlockSpec((HB, DK, DV), lambda h, c: (h, 0, 0)),
        ],
        out_shape=[jax.ShapeDtypeStruct((S, H * DV), BF16), jax.ShapeDtypeStruct((H, DK, DV), F32)],
        compiler_params=_cparams("parallel", "arbitrary"),
        name="retention_prompt",
    )(rq, rk, rv, rg, dec, cd, kd, gc)


def _retention_tables(log_gamma):
    H, DK, DV, C = RET_HEADS, RET_DK, RET_DV, RET_CHUNK
    i = jnp.arange(C, dtype=F32)
    rel = i[:, None] - i[None, :]
    causal = rel >= 0
    dec = jnp.where(causal[None], jnp.exp(jnp.where(causal, rel, 0.0)[None] * log_gamma[:, None, None]), 0.0)
    cross = jnp.exp((i[:, None] + 1.0) * log_gamma[None, :])
    kdec = jnp.exp((C - 1.0 - i)[:, None] * log_gamma[None, :])
    gC = jnp.exp(C * log_gamma)
    cd = jnp.broadcast_to(cross.T[:, :, None], (H, C, DV))
    kd = jnp.broadcast_to(kdec.T[:, :, None], (H, C, DK))
    gc = jnp.broadcast_to(gC[:, None, None], (H, 1, DV))
    return dec, cd, kd, gc


def _ret_sample_kernel(st_ref, qt_ref, kt_ref, v_ref, rg_ref, g1_ref, o_ref, ns_ref, *, H):
    qt = qt_ref[0]
    kt = kt_ref[0]
    v = v_ref[0]
    rg = rg_ref[0]
    for h in range(H):
        st = st_ref[0, 0, h]
        qc = qt[:, h:h + 1]
        kc = kt[:, h:h + 1]
        vr = v[h:h + 1, :]
        g1 = g1_ref[h]
        cross = jnp.sum(st * qc, axis=0, keepdims=True)
        qk = jnp.sum(qc * kc, axis=0, keepdims=True)
        o = qk * vr + cross * g1
        ns_ref[0, 0, h] = g1 * st + kc * vr
        ms = jnp.mean(o * o, axis=-1, keepdims=True)
        o_ref[0, h:h + 1, :] = o * lax.rsqrt(ms + EPS) * _silu(rg[h:h + 1, :])


def _retention_sample(state_ret, layer, rq, rk, rv, rg, log_gamma):
    DB = rq.shape[0]
    H, DK, DV = RET_HEADS, RET_DK, RET_DV
    qt = rq.reshape(DB, H, DK).transpose(0, 2, 1)
    kt = rk.reshape(DB, H, DK).transpose(0, 2, 1)
    g1 = jnp.broadcast_to(jnp.exp(1.0 * log_gamma)[:, None, None], (H, 1, DV))
    kern = functools.partial(_ret_sample_kernel, H=H)
    yr, ns = pl.pallas_call(
        kern,
        grid=(DB,),
        in_specs=[
            pl.BlockSpec((1, 1, H, DK, DV), lambda b: (layer, b, 0, 0, 0)),
            pl.BlockSpec((1, DK, H), lambda b: (b, 0, 0)),
            pl.BlockSpec((1, DK, H), lambda b: (b, 0, 0)),
            pl.BlockSpec((1, H, DV), lambda b: (b, 0, 0)),
            pl.BlockSpec((1, H, DV), lambda b: (b, 0, 0)),
            pl.BlockSpec((H, 1, DV), lambda b: (0, 0, 0)),
        ],
        out_specs=[
            pl.BlockSpec((1, H, DV), lambda b: (b, 0, 0)),
            pl.BlockSpec((1, 1, H, DK, DV), lambda b: (0, b, 0, 0, 0)),
        ],
        out_shape=[jax.ShapeDtypeStruct((DB, H, DV), F32),
                   jax.ShapeDtypeStruct((1, DB, H, DK, DV), state_ret.dtype)],
        compiler_params=_cparams("parallel"),
        name="retention_sample",
    )(state_ret, qt, kt, rv.reshape(DB, H, DV), rg.reshape(DB, H, DV), g1)
    return yr.reshape(DB, H * DV), ns[0]


def _sortable_key(x):
    bits = pltpu.bitcast(x, I32)
    return bits ^ ((bits >> 31) & 0x7FFFFFFF)


def _topk_select(keys_ref, nvis, ksel, col_bits, half_ref=None):
    _, R, CW = keys_ref.shape
    ksel_f = float(ksel)

    def lane_fold(f, acc):
        for g in range(CW // LANES):
            acc = acc + f[:, g * LANES:(g + 1) * LANES]
        return acc

    def count(pred):
        body = lambda c, acc: lane_fold(jnp.where(pred(keys_ref[c], c), 1.0, 0.0), acc)
        return jnp.sum(lax.fori_loop(0, nvis, body, jnp.zeros((R, LANES), F32)), axis=1, keepdims=True)

    if half_ref is None:
        def value_bit(bi, u):
            cand_u = u | lax.shift_left(jnp.int32(1), 31 - bi)
            cand = cand_u ^ INT_MIN
            cnt = count(lambda key, c: key >= cand)
            return jnp.where(cnt >= ksel_f, cand_u, u)

        T = lax.fori_loop(0, 32, value_bit, jnp.zeros((R, 1), I32)) ^ INT_MIN
    else:
        I16 = jnp.int16
        HALF = 1 << 15

        def row16(v):
            return jnp.concatenate([jnp.broadcast_to(v, (R, LANES)).astype(I16)] * (CW // LANES), axis=1)

        def count16(pred):
            one, zero = jnp.ones((R, CW), I16), jnp.zeros((R, CW), I16)
            body = lambda c, acc: lane_fold(jnp.where(pred(half_ref[c]), one, zero), acc)
            acc = lax.fori_loop(0, nvis, body, jnp.zeros((R, LANES), I16))
            return jnp.sum(acc.astype(F32), axis=1, keepdims=True)

        def search16(target):
            def bit(bi, u):
                cand_u = u | lax.shift_left(jnp.int32(1), 15 - bi)
                cand = row16(cand_u - HALF)
                cnt = count16(lambda half: half >= cand)
                return jnp.where(cnt >= target, cand_u, u)
            return lax.fori_loop(0, 16, bit, jnp.zeros((R, 1), I32))

        def fill_high(c, carry):
            half_ref[c] = (keys_ref[c] >> 16).astype(I16)
            return carry

        lax.fori_loop(0, nvis, fill_high, 0)
        hi = search16(ksel_f) - HALF
        hi16 = row16(hi)
        above = count16(lambda half: half > hi16)

        def fill_low(c, carry):
            low = ((keys_ref[c] & 0xFFFF) - HALF).astype(I16)
            half_ref[c] = jnp.where(half_ref[c] == hi16, low, jnp.full((R, CW), -HALF, I16))
            return carry

        lax.fori_loop(0, nvis, fill_low, 0)
        T = lax.shift_left(hi, 16) | search16(ksel_f - above)

    need = ksel_f - count(lambda key, c: key > T)
    ties = count(lambda key, c: key == T)

    def col_of(c):
        return c * CW + lax.broadcasted_iota(I32, (R, CW), 1)

    def col_search():
        def col_bit(bi, x):
            cand = x | lax.shift_left(jnp.int32(1), col_bits - 1 - bi)
            cnt = count(lambda key, c: (key == T) & (col_of(c) < cand))
            return jnp.where(cnt < need, cand, x)
        return lax.fori_loop(0, col_bits, col_bit, jnp.zeros((R, 1), I32))

    contested = jnp.max(jnp.where((ties > need) & (T != INT_MIN), 1.0, 0.0)) > 0.0
    J = lax.cond(contested, col_search, lambda: jnp.full((R, 1), 2 ** 30, I32))
    J = jnp.where(T == INT_MIN, -1, J)
    return T, J


def _dsa_select_kernel(iq_ref, iw_ref, ik_ref, bias_ref, keys_ref, half_ref, *, ksel, col_bits):
    NC, QB, CW = keys_ref.shape
    i = pl.program_id(0)
    nvis = ((i + 1) * QB + CW - 1) // CW
    w = iw_ref[...] * (IDX_HEADS ** -0.5 * IDX_DIM ** -0.5)
    rowpos = i * QB + lax.broadcasted_iota(I32, (QB, 1), 0)

    def col_of(c):
        return c * CW + lax.broadcasted_iota(I32, (QB, CW), 1)

    def score_chunk(c, carry):
        kt = ik_ref[c]
        col = c * CW + lax.broadcasted_iota(I32, (Q_BLOCK, CW), 1)
        for g in range(QB // Q_BLOCK):
            rows = slice(g * Q_BLOCK, (g + 1) * Q_BLOCK)
            s = _dot(iq_ref[0, g], kt)
            acc = jnp.zeros((Q_BLOCK, CW), F32)
            for h in range(IDX_HEADS):
                acc = acc + jnp.maximum(s[h * Q_BLOCK:(h + 1) * Q_BLOCK], 0.0) * w[rows, h:h + 1]
            keys_ref[c, rows, :] = jnp.where(col <= rowpos[rows], _sortable_key(acc), INT_MIN)
        return carry

    lax.fori_loop(0, nvis, score_chunk, 0)
    T, J = _topk_select(keys_ref, nvis, ksel, col_bits, half_ref)

    def emit(c, carry):
        key = keys_ref[c]
        col = col_of(c)
        sel = ((key > T) | ((key == T) & (col <= J))) & (col <= rowpos)
        bias_ref[0, c] = jnp.where(sel, 0.0, NEG_BIAS).astype(bias_ref.dtype)
        return carry

    lax.fori_loop(0, nvis, emit, 0)

    def fill(c, carry):
        bias_ref[0, c] = jnp.full((QB, CW), NEG_BIAS, bias_ref.dtype)
        return carry

    lax.fori_loop(nvis, NC, fill, 0)


def _dsa_select_prompt(iq_slabs, iw, ik, ksel, QB, CW):
    S = iw.shape[0]
    Hi, Di = IDX_HEADS, IDX_DIM
    assert Q_BLOCK == LANES and Di == LANES
    NQ, NC, G = S // QB, S // CW, QB // Q_BLOCK
    iq_stacked = iq_slabs.reshape(NQ, G, Hi * Q_BLOCK, Di)
    ik_cols = ik.reshape(NC, CW, Di).transpose(0, 2, 1)
    kern = functools.partial(_dsa_select_kernel, ksel=ksel, col_bits=int(np.ceil(np.log2(S))))
    return pl.pallas_call(
        kern,
        grid=(NQ,),
        in_specs=[
            pl.BlockSpec((1, G, Hi * Q_BLOCK, Di), lambda i: (i, 0, 0, 0)),
            pl.BlockSpec((QB, iw.shape[1]), lambda i: (i, 0)),
            pl.BlockSpec((NC, Di, CW), lambda i: (0, 0, 0)),
        ],
        out_specs=pl.BlockSpec((1, NC, QB, CW), lambda i: (i, 0, 0, 0)),
        out_shape=jax.ShapeDtypeStruct((NQ, NC, QB, CW), BF16),
        scratch_shapes=[pltpu.VMEM((NC, QB, CW), I32), pltpu.VMEM((NC, QB, CW), jnp.int16)],
        compiler_params=_cparams("parallel"),
        name="dsa_select_prompt",
    )(iq_stacked, iw, ik_cols)


def _dsa_attn_kernel(qi_ref, kj_ref, fin_ref, q_ref, kt_ref, v_ref, b_ref, o_ref, m_sc, l_sc, acc_sc, s_sc, a_sc,
                     *, H, DH, TK, SUB):
    step = pl.program_id(0)

    @pl.when(kj_ref[step] == 0)
    def _():
        m_sc[...] = jnp.full(m_sc.shape, NEG_BIAS, F32)
        l_sc[...] = jnp.zeros_like(l_sc)
        acc_sc[...] = jnp.zeros_like(acc_sc)

    ones = jnp.ones((SUB, LANES), BF16)
    for t in range(TK // SUB):
        keys = slice(t * SUB, (t + 1) * SUB)
        for h in range(H):
            cols = slice(h * DH, (h + 1) * DH)
            s_sc[h] = _dot(q_ref[:, cols], kt_ref[cols, keys]).astype(BF16) + b_ref[0, 0, :, keys]
        for h in range(H):
            s = s_sc[h]
            m_prev = m_sc[h]
            m_new = jnp.maximum(m_prev, jnp.max(s, axis=-1, keepdims=True).astype(F32))
            a_sc[h] = jnp.exp2(m_prev - m_new)
            s_sc[h] = jnp.exp2(s - jnp.concatenate([m_new.astype(BF16)] * (SUB // LANES), axis=1))
            m_sc[h] = m_new
        for h in range(H):
            cols = slice(h * DH, (h + 1) * DH)
            pv = _dot(s_sc[h], jnp.concatenate([v_ref[keys, cols], ones], axis=1))
            alpha = a_sc[h]
            l_sc[h] = alpha * l_sc[h] + pv[:, DH:]
            acc_sc[:, cols] = alpha * acc_sc[:, cols] + pv[:, :DH]

    @pl.when(fin_ref[step] == 1)
    def _():
        for h in range(H):
            cols = slice(h * DH, (h + 1) * DH)
            o_ref[:, cols] = (acc_sc[:, cols] / l_sc[h]).astype(o_ref.dtype)


def _dsa_attn_prompt(aq, ak_t, av, bias, QB, TK):
    S, W = aq.shape
    H, DH = ATT_HEADS, ATT_DH
    assert DH == LANES
    bq = bias.shape[2] // QB
    pairs = [(i, j) for i in range(S // QB) for j in range(((i + 1) * QB - 1) // TK + 1)]
    qi = jnp.asarray([p[0] for p in pairs], I32)
    kj = jnp.asarray([p[1] for p in pairs], I32)
    fin = jnp.asarray([int(p[1] == ((p[0] + 1) * QB - 1) // TK) for p in pairs], I32)
    kern = functools.partial(_dsa_attn_kernel, H=H, DH=DH, TK=TK, SUB=_pick(TK, (512, 256, 128)))
    return pl.pallas_call(
        kern,
        grid_spec=pltpu.PrefetchScalarGridSpec(
            num_scalar_prefetch=3,
            grid=(len(pairs),),
            in_specs=[
                pl.BlockSpec((QB, W), lambda s, qi, kj, fin: (qi[s], 0)),
                pl.BlockSpec((W, TK), lambda s, qi, kj, fin: (0, kj[s])),
                pl.BlockSpec((TK, W), lambda s, qi, kj, fin: (kj[s], 0)),
                pl.BlockSpec((1, 1, QB, TK), lambda s, qi, kj, fin: (qi[s] // bq, kj[s], qi[s] % bq, 0)),
            ],
            out_specs=pl.BlockSpec((QB, W), lambda s, qi, kj, fin: (qi[s], 0)),
            scratch_shapes=[pltpu.VMEM((H, QB, LANES), F32), pltpu.VMEM((H, QB, LANES), F32),
                            pltpu.VMEM((QB, W), F32), pltpu.VMEM((H, QB, _pick(TK, (512, 256, 128))), BF16),
                            pltpu.VMEM((H, QB, LANES), F32)],
        ),
        out_shape=jax.ShapeDtypeStruct((S, W), BF16),
        compiler_params=_cparams("arbitrary"),
        name="dsa_attn_prompt",
    )(qi, kj, fin, aq, ak_t, av, bias)


def _idx_sample_kernel(pt_ref, iq_ref, w_ref, *refs):
    page_refs, o_ref = refs[:-1], refs[-1]
    iq = iq_ref[0].astype(BF16)
    w = w_ref[0]
    rows = []
    for page_ref in page_refs:
        s = _dot_nt(iq, page_ref[0, 0].astype(BF16))
        rows.append(jnp.sum(jnp.maximum(s, 0.0) * w, axis=0, keepdims=True))
    o_ref[0, 0] = jnp.concatenate(rows, axis=1)


def _idx_sample(cache_idx_k, layer, page_table, iq, iw):
    DB, NP = page_table.shape
    P, Di = cache_idx_k.shape[2:]
    Hi = IDX_HEADS
    G = _pick(NP, (32, 16, 8, 4, 2, 1))
    w = jnp.broadcast_to((iw * (IDX_HEADS ** -0.5 * IDX_DIM ** -0.5))[:, :, None], (DB, Hi, P))

    def page_spec(r):
        return pl.BlockSpec((1, 1, P, Di), lambda b, g, pt: (layer, pt[b, g * G + r], 0, 0))

    out = pl.pallas_call(
        _idx_sample_kernel,
        grid_spec=pltpu.PrefetchScalarGridSpec(
            num_scalar_prefetch=1,
            grid=(DB, NP // G),
            in_specs=[
                pl.BlockSpec((1, Hi, Di), lambda b, g, pt: (b, 0, 0)),
                pl.BlockSpec((1, Hi, P), lambda b, g, pt: (b, 0, 0)),
            ] + [page_spec(r) for r in range(G)],
            out_specs=pl.BlockSpec((1, 1, 1, G * P), lambda b, g, pt: (b, g, 0, 0)),
        ),
        out_shape=jax.ShapeDtypeStruct((DB, NP // G, 1, G * P), F32),
        compiler_params=_cparams("parallel", "arbitrary"),
        name="dsa_index_sample",
    )(page_table, iq.reshape(DB, Hi, Di), w, *([cache_idx_k] * G))
    return out.reshape(DB, NP * P)


def _select_sample_kernel(sc_ref, iq_ref, ikn_ref, iw_ref, idx_ref, keys_ref, cnt_ref, *, ksel, past_len,
                          col_bits):
    NC, R, CW = keys_ref.shape
    NP = NC - 1
    w = iw_ref[...] * (IDX_HEADS ** -0.5 * IDX_DIM ** -0.5)
    ikn = ikn_ref[...]
    new = jnp.zeros((R, 1), F32)
    for h in range(IDX_HEADS):
        s = jnp.sum(iq_ref[:, h * IDX_DIM:(h + 1) * IDX_DIM] * ikn, axis=-1, keepdims=True)
        new = new + jnp.maximum(s, 0.0) * w[:, h:h + 1]

    def col_of(c):
        return c * CW + lax.broadcasted_iota(I32, (R, CW), 1)

    def load(c, carry):
        keys_ref[c] = _sortable_key(sc_ref[c] + 0.0)
        return carry

    lax.fori_loop(0, NP, load, 0)
    keys_ref[NP] = jnp.where(col_of(NP) <= past_len, _sortable_key(jnp.broadcast_to(new, (R, CW))), INT_MIN)
    T, J = _topk_select(keys_ref, NC, ksel, col_bits)

    tri = (lax.broadcasted_iota(I32, (CW, CW), 0) <= lax.broadcasted_iota(I32, (CW, CW), 1)).astype(BF16)

    def prefix(c, before):
        key = keys_ref[c]
        col = col_of(c)
        sel = ((key > T) | ((key == T) & (col <= J))) & (col <= past_len)
        f = jnp.where(sel, 1.0, 0.0)
        cnt_ref[c] = _dot(f.astype(BF16), tri) + before
        return before + jnp.sum(f, axis=-1, keepdims=True)

    lax.fori_loop(0, NC, prefix, jnp.zeros((R, 1), F32))

    jcol = lax.broadcasted_iota(I32, (ksel, 1), 0).astype(F32)
    lane = lax.broadcasted_iota(I32, (ksel, LANES), 1)

    def one_row(r, out):
        def chunk(c, acc):
            return acc + jnp.where(cnt_ref[c, pl.ds(r, 1), :] <= jcol, 1.0, 0.0)
        acc = lax.fori_loop(0, NC, chunk, jnp.zeros((ksel, CW), F32))
        return jnp.where(lane == r, jnp.sum(acc, axis=-1, keepdims=True), out)

    idx_ref[...] = lax.fori_loop(0, R, one_row, jnp.zeros((ksel, LANES), F32)).astype(I32)


def _select_sample(scores, iq, ik_new, iw, ksel):
    DB = scores.shape[0]
    P = LANES
    NP = scores.shape[1] // P
    past_len = NP * P
    assert DB <= LANES
    sc = scores.reshape(DB, NP, P).transpose(1, 0, 2)
    kern = functools.partial(_select_sample_kernel, ksel=ksel, past_len=past_len,
                             col_bits=int(np.ceil(np.log2(past_len + P))))
    idx_t = pl.pallas_call(
        kern,
        out_shape=jax.ShapeDtypeStruct((ksel, LANES), I32),
        scratch_shapes=[pltpu.VMEM((NP + 1, DB, P), I32), pltpu.VMEM((NP + 1, DB, P), F32)],
        compiler_params=pltpu.CompilerParams(vmem_limit_bytes=VMEM_LIMIT_BYTES),
        name="dsa_select_sample",
    )(sc, iq, ik_new, iw)
    return idx_t[:, :DB].T


def _attn_sample_kernel(idx_ref, pt_ref, q_ref, kn_ref, vn_ref, ck_hbm, cv_hbm, o_ref, kbuf, vbuf, sem,
                        m_sc, l_sc, acc_sc, *, layer, G, ksel, past_len, page):
    b = pl.program_id(0)
    slot = b % 2
    scale = q_ref.shape[-1] ** -0.5

    def row_copies(seq, r, sl):
        pos = jnp.minimum(idx_ref[seq * ksel + r], past_len - 1)
        src = (layer, pt_ref[seq, pos // page], pos % page)
        return (pltpu.make_async_copy(ck_hbm.at[src], kbuf.at[sl, r], sem.at[sl, 0]),
                pltpu.make_async_copy(cv_hbm.at[src], vbuf.at[sl, r], sem.at[sl, 1]))

    def start_rows(seq, sl):
        def body(r, carry):
            for cp in row_copies(seq, r, sl):
                cp.start()
            return carry
        lax.fori_loop(0, ksel, body, 0)

    @pl.when(b == 0)
    def _():
        start_rows(0, 0)

    @pl.when(b + 1 < pl.num_programs(0))
    def _():
        start_rows(b + 1, 1 - slot)

    def wait_row(r, carry):
        for cp in row_copies(b, r, slot):
            cp.wait()
        return carry

    lax.fori_loop(0, ksel, wait_row, 0)

    m_sc[...] = jnp.full(m_sc.shape, NEG_BIAS, F32)
    l_sc[...] = jnp.zeros_like(l_sc)
    acc_sc[...] = jnp.zeros_like(acc_sc)
    q = q_ref[0]

    def slot_bias(r):
        pos = idx_ref[b * ksel + r]
        return jnp.full((1, 1, 1), jnp.where(pos >= past_len, NEG_BIAS, 0.0), F32)

    def update(kk, vv, bias3):
        s = jnp.sum(kk * q[None], axis=-1, keepdims=True) * scale + bias3
        m_prev = m_sc[...]
        m_new = jnp.maximum(m_prev, jnp.max(s, axis=0))
        alpha = jnp.exp(m_prev - m_new)
        pr = jnp.exp(s - m_new[None])
        l_sc[...] = alpha * l_sc[...] + jnp.sum(pr, axis=0)
        acc_sc[...] = alpha * acc_sc[...] + jnp.sum(pr * vv, axis=0)
        m_sc[...] = m_new

    for c in range(ksel // G):
        rows = pl.ds(c * G, G)
        update(kbuf[slot, rows], vbuf[slot, rows],
               jnp.concatenate([slot_bias(c * G + r) for r in range(G)], axis=0))

    last = idx_ref[b * ksel + ksel - 1]
    update(kn_ref[...], vn_ref[...], jnp.full((1, 1, 1), jnp.where(last >= past_len, 0.0, NEG_BIAS), F32))
    o_ref[0] = acc_sc[...] / l_sc[...]


def _attn_sample(cache_k, cache_v, layer, page_table, aq, ak_new, av_new, idx):
    DB, NP = page_table.shape
    P, H, DH = cache_k.shape[2:]
    ksel = idx.shape[1]
    vec_spec = pl.BlockSpec((1, H, DH), lambda b, idx_ref, pt: (b, 0, 0))
    kern = functools.partial(_attn_sample_kernel, layer=layer, G=_pick(ksel, (32, 16, 8, 4, 2, 1)), ksel=ksel,
                             past_len=NP * P, page=P)
    out = pl.pallas_call(
        kern,
        grid_spec=pltpu.PrefetchScalarGridSpec(
            num_scalar_prefetch=2,
            grid=(DB,),
            in_specs=[vec_spec, vec_spec, vec_spec, pl.BlockSpec(memory_space=pl.ANY),
                      pl.BlockSpec(memory_space=pl.ANY)],
            out_specs=vec_spec,
            scratch_shapes=[pltpu.VMEM((2, ksel, H, DH), F32), pltpu.VMEM((2, ksel, H, DH), F32),
                            pltpu.SemaphoreType.DMA((2, 2)),
                            pltpu.VMEM((H, 1), F32), pltpu.VMEM((H, 1), F32), pltpu.VMEM((H, DH), F32)],
        ),
        out_shape=jax.ShapeDtypeStruct((DB, H, DH), F32),
        compiler_params=_cparams("arbitrary"),
        name="dsa_attn_sample",
    )(idx.reshape(-1), page_table, aq.reshape(DB, H, DH), ak_new.reshape(DB, H, DH), av_new.reshape(DB, H, DH),
      cache_k, cache_v)
    return out.reshape(DB, H * DH)


def _rotary_tables(pos):
    d = RET_DK
    inv_freq = 1.0 / (ROPE_BASE ** jnp.linspace(0.0, 1.0, d // 2, dtype=F32))
    ang = pos.astype(F32)[:, None] * inv_freq[None, :]
    cos = jnp.repeat(jnp.cos(ang), 2, axis=1)
    sin = jnp.sin(ang)
    sin_signed = jnp.stack([-sin, sin], axis=-1).reshape(pos.shape[0], d)
    return cos, sin_signed


def _in_proj_columns(w_in):
    rqk, rv_w = RET_HEADS * RET_DK, RET_HEADS * RET_DV
    aw, iqw = ATT_HEADS * ATT_DH, IDX_HEADS * IDX_DIM
    d_model = w_in.shape[1]
    names = ("rq", "rk", "rv", "rg", "aq", "ak", "av", "iq", "ik", "iw", "ga", "gb")
    sizes = (rqk, rqk, rv_w, rv_w, aw, aw, aw, iqw, IDX_DIM, IDX_HEADS, d_model, d_model)
    offs = np.concatenate([[0], np.cumsum(sizes)]).tolist()
    assert offs[-1] == w_in.shape[2]
    return {n: (offs[t], sizes[t]) for t, n in enumerate(names)}


def _mix_inputs(hp, hs, w_in_t, layer, cols, qg, kg, rot_p, rot_s, q_scale):
    tm = _pick(hp.shape[0], ROW_TILES)
    P, Sm = {}, {}

    def proj(name, epi_p, ex_p, out_p, epi_s, ex_s, out_s, tn_prefs=(512, 256, 128), n=None):
        col0 = cols[name][0]
        n = n or cols[name][1]
        outs = _mm([hp], [(0, w_in_t, layer, col0, n, True)], ex_p, epi_p, [d for _, d in out_p], tm=tm,
                   tn=_pick(n, tn_prefs), name="proj_" + name, rider=([hs], ex_s, epi_s, [d for _, d in out_s]))
        for (key, _), o in zip(out_p, outs):
            P[key] = o
        for (key, _), o in zip(out_s, outs[len(out_p):]):
            Sm[key] = o

    def same(name, dt_p, **kw):
        proj(name, _epi_identity, [], [(name, dt_p)], _epi_identity, [], [(name, F32)], **kw)

    rot_p = [(t, "rows") for t in rot_p]
    rot_s = [(t, "rows") for t in rot_s]
    for name, scale in (("rq", 1.0), ("rk", RET_DK ** -0.5)):
        epi = functools.partial(_epi_rotary, scale)
        proj(name, epi, rot_p, [(name, BF16)], epi, rot_s, [(name, F32)])
    same("rv", BF16)
    same("rg", F32)
    norm1, norm2 = functools.partial(_epi_head_norm, 1), functools.partial(_epi_head_norm, 2)
    proj("aq", norm1, [(qg.reshape(1, -1) * q_scale, "full")], [("aq", BF16)],
         norm1, [(qg.reshape(1, -1), "full")], [("aq", F32)])
    kgain = [(kg.reshape(1, -1), "full")]
    proj("ak", norm2, kgain, [("ak", F32), ("ak16t", (BF16, "T"))], norm1, kgain, [("ak", F32)])
    proj("av", _epi_two_copies, [], [("av", F32), ("av16", BF16)], _epi_identity, [], [("av", F32)])
    same("iq", (BF16, "slabs"))
    same("ik", F32, tn_prefs=(2 * LANES,), n=2 * LANES)
    same("ga", F32)
    same("gb", F32)
    for d in (P, Sm):
        ikw = d["ik"]
        d["ik"] = ikw[:, :IDX_DIM]
        d["iw"] = ikw[:, IDX_DIM:IDX_DIM + IDX_HEADS]
    return P, Sm


def _merge_ffn(xp, xs, mp, ms, layer, w_ret_o, w_att_o, w_out, g_ffn, w_gate, w_up, wd):
    M, D = xp.shape
    F = w_gate.shape[2]
    tm = _pick(M, ROW_TILES)
    gates = lambda t: [(t[2], "tile"), (t[3], "tile")]
    m_p, m_s = _mm(list(mp[:2]), [(0, w_ret_o, layer, 0, D, False), (1, w_att_o, layer, 0, D, False)], gates(mp),
                   _epi_merge, [BF16], tm=_pick(M, ROW_TILES[1:]), tn=_pick(D, (512, 256, 128)), name="merge",
                   rider=(list(ms[:2]), gates(ms), _epi_merge, [BF16]))
    x1p, x1s = _mm([m_p], [(0, w_out, layer, 0, D, False)], [(xp, "tile")], _epi_residual, [F32], tm=tm,
                   tn=_pick(D, (512, 256, 128)), name="out_proj",
                   rider=([m_s], [(xs, "tile")], _epi_residual, [F32]))
    hfp, hfs = _rmsnorm(x1p, g_ffn), _rmsnorm(x1s, g_ffn)
    up, us = _mm([hfp], [(0, w_gate, layer, 0, F, False), (0, w_up, layer, 0, F, False)], [], _epi_swiglu, [BF16],
                 tm=tm, tn=_pick(F, (256, 128)), name="ffn_up", rider=([hfs], [], _epi_swiglu, [BF16]))
    down = lambda u, x1: _mm([u], [(0, wd, None, 0, D, False)], [(x1, "tile")], _epi_residual, [F32],
                             tm=_pick(u.shape[0], ROW_TILES[1:]), tn=_pick(D, (256, 128)), name="ffn_down",
                             hold_b=False)[0]
    return down(up, x1p), down(us, x1s)


def kernel(x_prompt, x_sample, cache_k, cache_v, cache_idx_k, state_ret, page_table, norm_mix_g, w_in,
           q_norm_g, k_norm_g, w_ret_o, w_att_o, w_out, norm_ffn_g, w_ffn_gate, w_ffn_up, w_ffn_down):
    depth = w_in.shape[0]
    B, S, D = x_prompt.shape
    DB, T, _ = x_sample.shape
    assert B == 1 and T == 1
    page_size = cache_k.shape[2]
    past_len = page_table.shape[1] * page_size

    log_gamma = jnp.log1p(-jnp.exp2(-5.0 - jnp.arange(RET_HEADS, dtype=F32)))
    ret_tabs = _retention_tables(log_gamma)
    cos_p, sin_p = _rotary_tables(jnp.arange(S, dtype=I32))
    cos_s, sin_s = _rotary_tables(jnp.full((DB,), past_len, I32))
    ksel_p = min(TOPK_MAX, S // 4)
    ksel_s = min(TOPK_MAX, (past_len + T) // 4)
    cw = _pick(S, (512, 256, 128))
    qb = _pick(S, (2 * Q_BLOCK, Q_BLOCK))
    cols = _in_proj_columns(w_in)
    w_in_t = jnp.swapaxes(w_in, 1, 2)

    xp = x_prompt.reshape(S, D)
    xs = x_sample.reshape(DB, D)
    kp, vp, ikp, stp, ksm, vsm, iks, sts = [], [], [], [], [], [], [], []
    for layer in range(depth):
        wd = w_ffn_down[layer].astype(BF16)
        dense = (layer, w_ret_o, w_att_o, w_out, norm_ffn_g[layer], w_ffn_gate, w_ffn_up, wd)

        p, s = _mix_inputs(_rmsnorm(xp, norm_mix_g[layer]), _rmsnorm(xs, norm_mix_g[layer]), w_in_t, layer, cols,
                           q_norm_g[layer], k_norm_g[layer], (cos_p, sin_p), (cos_s, sin_s),
                           ATT_DH ** -0.5 * float(np.log2(np.e)))

        yr_p, st_p = _retention_prompt(p["rq"], p["rk"], p["rv"], p["rg"], ret_tabs)
        bias = _dsa_select_prompt(p["iq"], p["iw"], p["ik"].astype(BF16), ksel_p, qb, cw)
        ya_p = _dsa_attn_prompt(p["aq"], p["ak16t"], p["av16"], bias, qb, cw)

        yr_s, st_s = _retention_sample(state_ret, layer, s["rq"], s["rk"], s["rv"], s["rg"], log_gamma)
        scores = _idx_sample(cache_idx_k, layer, page_table, s["iq"], s["iw"])
        sel_idx = _select_sample(scores, s["iq"], s["ik"], s["iw"], ksel_s)
        ya_s = _attn_sample(cache_k, cache_v, layer, page_table, s["aq"], s["ak"], s["av"], sel_idx)

        xp, xs = _merge_ffn(xp, xs, (yr_p, ya_p, p["ga"], p["gb"]),
                            (yr_s.astype(BF16), ya_s.astype(BF16), s["ga"], s["gb"]), *dense)
        kp.append(p["ak"].reshape(B, S, ATT_HEADS, ATT_DH))
        vp.append(p["av"].reshape(B, S, ATT_HEADS, ATT_DH))
        ikp.append(p["ik"].reshape(B, S, IDX_DIM))
        stp.append(st_p[None].astype(state_ret.dtype))
        ksm.append(s["ak"].reshape(DB, T, ATT_HEADS, ATT_DH))
        vsm.append(s["av"].reshape(DB, T, ATT_HEADS, ATT_DH))
        iks.append(s["ik"].reshape(DB, T, IDX_DIM))
        sts.append(st_s)
    return (xp.reshape(B, S, D), xs.reshape(DB, T, D), jnp.stack(kp), jnp.stack(vp), jnp.stack(ikp),
            jnp.stack(stp), jnp.stack(ksm), jnp.stack(vsm), jnp.stack(iks), jnp.stack(sts))
```

```python
import functools

import numpy as np
import jax
import jax.numpy as jnp
from jax import lax
from jax.experimental import pallas as pl
from jax.experimental.pallas import tpu as pltpu

F32 = jnp.float32
BF16 = jnp.bfloat16
I32 = jnp.int32

RET_HEADS = 8
RET_DK = 256
RET_DV = 512
RET_CHUNK = 128
ROPE_BASE = 10000.0
ATT_HEADS = 16
ATT_DH = 128
IDX_HEADS = 8
IDX_DIM = 128
TOPK_MAX = 256
Q_BLOCK = 128
EPS = 1e-6

LANES = 128
VMEM_LIMIT_BYTES = 56 * 1024 * 1024

ROW_TILES = (1024, 512, 256, 128, 64, 32, 16, 8)

NEG_BIAS = -1e30
INT_MIN = -(2 ** 31)


def _cparams(*sem):
    return pltpu.CompilerParams(dimension_semantics=sem, vmem_limit_bytes=VMEM_LIMIT_BYTES)


def _pick(n, prefs):
    for p in prefs:
        if p <= n and n % p == 0:
            return p
    return n


def _dot(a, b):
    return jnp.dot(a, b, preferred_element_type=F32)


def _dot_nt(a, b):
    return lax.dot_general(a, b, (((1,), (1,)), ((), ())), preferred_element_type=F32)


def _sigmoid(x):
    return 1.0 / (1.0 + jnp.exp(-x))


def _silu(x):
    return x * _sigmoid(x)


def _rmsnorm_kernel(x_ref, g_ref, o_ref):
    x = x_ref[...]
    ms = jnp.mean(x * x, axis=-1, keepdims=True)
    o_ref[...] = (x * lax.rsqrt(ms + EPS) * g_ref[...]).astype(o_ref.dtype)


def _rmsnorm(x, g):
    M, D = x.shape
    tm = _pick(M, (256, 128, 64, 32, 16, 8))
    return pl.pallas_call(
        _rmsnorm_kernel,
        grid=(M // tm,),
        in_specs=[pl.BlockSpec((tm, D), lambda i: (i, 0)), pl.BlockSpec((1, D), lambda i: (0, 0))],
        out_specs=pl.BlockSpec((tm, D), lambda i: (i, 0)),
        out_shape=jax.ShapeDtypeStruct((M, D), BF16),
        compiler_params=_cparams("parallel"),
        name="rmsnorm",
    )(x, g.reshape(1, D))


def _mm_kernel(*refs, n_a, b_src, b_nt, n_e, o_kinds, cast, epilogue, rider):
    n_o = len(o_kinds)
    n_b = len(b_src)
    take = lambda n, it=iter(refs): [next(it) for _ in range(n)]
    a_refs, b_refs, e_refs = take(n_a), take(n_b), take(n_e)
    if rider:
        n_e2, n_o2, epilogue2 = rider
        a2_refs, e2_refs = take(n_a), take(n_e2)
    o_refs = take(n_o)
    if rider:
        o2_refs = take(n_o2)
    if cast:
        w_refs = take(n_b)

    def compute(a_rs, e_rs, o_rs, epi, kinds):
        bs = w_refs if cast else b_refs
        accs = [(_dot_nt if nt else _dot)(a_rs[s][...], b[...]) for s, nt, b in zip(b_src, b_nt, bs)]
        for o, r, kind in zip(o_rs, epi(accs, [e[...] for e in e_rs]), kinds):
            if kind == "slabs":
                for rg in range(o.shape[0]):
                    for cg in range(o.shape[1]):
                        o[rg, cg] = r[rg * LANES:(rg + 1) * LANES, cg * LANES:(cg + 1) * LANES].astype(o.dtype)
            elif kind == "T":
                o[...] = r.T.astype(o.dtype)
            else:
                o[...] = r.astype(o.dtype)

    if cast or rider:
        @pl.when(pl.program_id(1) == 0)
        def _():
            if cast:
                for b, w in zip(b_refs, w_refs):
                    w[...] = b[...].reshape(w.shape).astype(w.dtype)
            if rider:
                compute(a2_refs, e2_refs, o2_refs, epilogue2, ("plain",) * n_o2)
    compute(a_refs, e_refs, o_refs, epilogue, o_kinds)


def _mm(a_list, b_list, extras, epilogue, out_dtypes, *, tm, tn, name, hold_b=True, rider=None):
    M = a_list[0].shape[0]
    N = b_list[0][4]
    cast = b_list[0][2] is not None
    assert all((b[2] is not None) == cast for b in b_list) and (hold_b or not (cast or rider))
    if hold_b:
        grid = (N // tn, M // tm)
        ij = lambda f: (lambda j, i: f(i, j))
    else:
        grid = (M // tm, N // tn)
        ij = lambda f: f
    in_specs, args, scratch = [], [], []
    for a in a_list:
        in_specs.append(pl.BlockSpec((tm, a.shape[1]), ij(lambda i, j: (i, 0))))
        args.append(a)
    for _, w, layer, col0, _, nt in b_list:
        if nt:
            K = w.shape[2]
            if col0 % tn == 0:
                spec = pl.BlockSpec((None, tn, K), ij(lambda i, j, l=layer, c=col0 // tn: (l, c + j, 0)))
            else:
                spec = pl.BlockSpec((pl.Element(1), pl.Element(tn), pl.Element(K)),
                                    ij(lambda i, j, l=layer, c=col0: (l, pl.multiple_of(c + j * tn, 8), 0)))
            scratch.append(pltpu.VMEM((tn, K), BF16))
        elif cast:
            assert col0 % tn == 0
            K = w.shape[1]
            spec = pl.BlockSpec((None, K, tn), ij(lambda i, j, l=layer, c=col0 // tn: (l, 0, c + j)))
            scratch.append(pltpu.VMEM((K, tn), BF16))
        else:
            assert col0 % tn == 0
            spec = pl.BlockSpec((w.shape[0], tn), ij(lambda i, j, c=col0 // tn: (0, c + j)))
        in_specs.append(spec)
        args.append(w)
    def add_extras(ex_list, rows, row_index):
        for arr, kind in ex_list:
            if kind == "tile":
                in_specs.append(pl.BlockSpec((rows, tn), ij(lambda i, j: (row_index(i), j))))
            elif kind == "rows":
                in_specs.append(pl.BlockSpec((rows, arr.shape[1]), ij(lambda i, j: (row_index(i), 0))))
            else:
                in_specs.append(pl.BlockSpec(arr.shape, ij(lambda i, j, nd=arr.ndim: (0,) * nd)))
            args.append(arr)

    add_extras(extras, tm, lambda i: i)
    out_specs, out_shape, o_kinds = [], [], []
    for od in out_dtypes:
        dt, kind = od if isinstance(od, tuple) else (od, "plain")
        o_kinds.append(kind)
        if kind == "T":
            out_specs.append(pl.BlockSpec((tn, tm), ij(lambda i, j: (j, i))))
            out_shape.append(jax.ShapeDtypeStruct((N, M), dt))
        elif kind == "slabs":
            out_specs.append(pl.BlockSpec((tm // LANES, tn // LANES, LANES, LANES), ij(lambda i, j: (i, j, 0, 0))))
            out_shape.append(jax.ShapeDtypeStruct((M // LANES, N // LANES, LANES, LANES), dt))
        else:
            out_specs.append(pl.BlockSpec((tm, tn), ij(lambda i, j: (i, j))))
            out_shape.append(jax.ShapeDtypeStruct((M, N), dt))
    if rider:
        a_list2, extras2, epilogue2, out_dtypes2 = rider
        M2 = a_list2[0].shape[0]
        for a in a_list2:
            in_specs.append(pl.BlockSpec((M2, a.shape[1]), ij(lambda i, j: (0, 0))))
            args.append(a)
        add_extras(extras2, M2, lambda i: 0)
        out_specs += [pl.BlockSpec((M2, tn), ij(lambda i, j: (0, j))) for _ in out_dtypes2]
        out_shape += [jax.ShapeDtypeStruct((M2, N), od) for od in out_dtypes2]
        rider = (len(extras2), len(out_dtypes2), epilogue2)
    kern = functools.partial(_mm_kernel, n_a=len(a_list), b_src=tuple(b[0] for b in b_list),
                             b_nt=tuple(b[5] for b in b_list), n_e=len(extras), o_kinds=tuple(o_kinds), cast=cast,
                             epilogue=epilogue, rider=rider)
    outs = pl.pallas_call(
        kern,
        grid=grid,
        in_specs=in_specs,
        out_specs=out_specs,
        out_shape=out_shape,
        scratch_shapes=scratch,
        compiler_params=_cparams("parallel", "arbitrary"),
        name=name,
    )(*args)
    return outs


def _epi_identity(accs, extras):
    return [accs[0]]


def _epi_two_copies(accs, extras):
    return [accs[0], accs[0]]


def _epi_rotary(scale, accs, extras):
    acc = accs[0]
    cos, sin_signed = extras
    dk = cos.shape[1]
    lane = lax.broadcasted_iota(I32, (acc.shape[0], dk), 1)
    even = (lane % 2) == 0
    outs = []
    for h in range(acc.shape[1] // dk):
        x = acc[:, h * dk:(h + 1) * dk]
        swapped = jnp.where(even, pltpu.roll(x, dk - 1, 1), pltpu.roll(x, 1, 1))
        outs.append((x * cos + swapped * sin_signed) * scale)
    return [jnp.concatenate(outs, axis=1) if len(outs) > 1 else outs[0]]


def _epi_head_norm(n_out, accs, extras):
    acc = accs[0]
    g = extras[0]
    dh = g.shape[1]
    outs = []
    for h in range(acc.shape[1] // dh):
        x = acc[:, h * dh:(h + 1) * dh]
        ms = jnp.mean(x * x, axis=-1, keepdims=True)
        outs.append(x * lax.rsqrt(ms + EPS) * g)
    y = jnp.concatenate(outs, axis=1) if len(outs) > 1 else outs[0]
    return [y] * n_out


def _epi_merge(accs, extras):
    ga, gb = extras
    return [_sigmoid(ga) * accs[0] + _sigmoid(gb) * accs[1]]


def _epi_residual(accs, extras):
    return [extras[0] + accs[0]]


def _epi_swiglu(accs, extras):
    return [_silu(accs[0]) * accs[1]]


def _ret_prompt_kernel(q_ref, k_ref, v_ref, rg_ref, dec_ref, cd_ref, kd_ref, gc_ref, yr_ref, st_ref,
                       *, nsub, C):
    @pl.when(pl.program_id(1) == 0)
    def _():
        st_ref[...] = jnp.zeros_like(st_ref)

    DK, DV = kd_ref.shape[2], cd_ref.shape[2]
    for t in range(nsub):
        rows = slice(t * C, (t + 1) * C)
        for g in range(st_ref.shape[0]):
            qk = slice(g * DK, (g + 1) * DK)
            vc = slice(g * DV, (g + 1) * DV)
            q = q_ref[rows, qk]
            k = k_ref[rows, qk]
            v = v_ref[rows, vc]
            st = st_ref[g]
            s = _dot_nt(q, k) * dec_ref[g]
            o = _dot(s.astype(BF16), v) + _dot(q, st.astype(BF16)) * cd_ref[g]
            kdt = (k.astype(F32) * kd_ref[g]).T.astype(BF16)
            st_ref[g] = gc_ref[g] * st + _dot(kdt, v)
            ms = jnp.mean(o * o, axis=-1, keepdims=True)
            yr_ref[rows, vc] = (o * lax.rsqrt(ms + EPS) * _silu(rg_ref[rows, vc])).astype(yr_ref.dtype)


def _retention_prompt(rq, rk, rv, rg, tabs):
    S = rq.shape[0]
    H, DK, DV, C = RET_HEADS, RET_DK, RET_DV, RET_CHUNK
    nsub = _pick(S // C, (4, 2, 1))
    T = nsub * C
    HB = _pick(H, (4, 2, 1))
    dec, cd, kd, gc = tabs
    kern = functools.partial(_ret_prompt_kernel, nsub=nsub, C=C)
    return pl.pallas_call(
        kern,
        grid=(H // HB, S // T),
        in_specs=[
            pl.BlockSpec((T, HB * DK), lambda h, c: (c, h)),
            pl.BlockSpec((T, HB * DK), lambda h, c: (c, h)),
            pl.BlockSpec((T, HB * DV), lambda h, c: (c, h)),
            pl.BlockSpec((T, HB * DV), lambda h, c: (c, h)),
            pl.BlockSpec((HB, C, C), lambda h, c: (h, 0, 0)),
            pl.BlockSpec((HB, C, DV), lambda h, c: (h, 0, 0)),
            pl.BlockSpec((HB, C, DK), lambda h, c: (h, 0, 0)),
            pl.BlockSpec((HB, 1, DV), lambda h, c: (h, 0, 0)),
        ],
        out_specs=[
            pl.BlockSpec((T, HB * DV), lambda h, c: (c, h)),
            pl.BlockSpec((HB, DK, DV), lambda h, c: (h, 0, 0)),
        ],
        out_shape=[jax.ShapeDtypeStruct((S, H * DV), BF16), jax.ShapeDtypeStruct((H, DK, DV), F32)],
        compiler_params=_cparams("parallel", "arbitrary"),
        name="retention_prompt",
    )(rq, rk, rv, rg, dec, cd, kd, gc)


def _retention_tables(log_gamma):
    H, DK, DV, C = RET_HEADS, RET_DK, RET_DV, RET_CHUNK
    i = jnp.arange(C, dtype=F32)
    rel = i[:, None] - i[None, :]
    causal = rel >= 0
    dec = jnp.where(causal[None], jnp.exp(jnp.where(causal, rel, 0.0)[None] * log_gamma[:, None, None]), 0.0)
    cross = jnp.exp((i[:, None] + 1.0) * log_gamma[None, :])
    kdec = jnp.exp((C - 1.0 - i)[:, None] * log_gamma[None, :])
    gC = jnp.exp(C * log_gamma)
    cd = jnp.broadcast_to(cross.T[:, :, None], (H, C, DV))
    kd = jnp.broadcast_to(kdec.T[:, :, None], (H, C, DK))
    gc = jnp.broadcast_to(gC[:, None, None], (H, 1, DV))
    return dec, cd, kd, gc


def _ret_sample_kernel(st_ref, qt_ref, kt_ref, v_ref, rg_ref, g1_ref, o_ref, ns_ref, *, H):
    qt = qt_ref[0]
    kt = kt_ref[0]
    v = v_ref[0]
    rg = rg_ref[0]
    for h in range(H):
        st = st_ref[0, 0, h]
        qc = qt[:, h:h + 1]
        kc = kt[:, h:h + 1]
        vr = v[h:h + 1, :]
        g1 = g1_ref[h]
        cross = jnp.sum(st * qc, axis=0, keepdims=True)
        qk = jnp.sum(qc * kc, axis=0, keepdims=True)
        o = qk * vr + cross * g1
        ns_ref[0, 0, h] = g1 * st + kc * vr
        ms = jnp.mean(o * o, axis=-1, keepdims=True)
        o_ref[0, h:h + 1, :] = o * lax.rsqrt(ms + EPS) * _silu(rg[h:h + 1, :])


def _retention_sample(state_ret, layer, rq, rk, rv, rg, log_gamma):
    DB = rq.shape[0]
    H, DK, DV = RET_HEADS, RET_DK, RET_DV
    qt = rq.reshape(DB, H, DK).transpose(0, 2, 1)
    kt = rk.reshape(DB, H, DK).transpose(0, 2, 1)
    g1 = jnp.broadcast_to(jnp.exp(1.0 * log_gamma)[:, None, None], (H, 1, DV))
    kern = functools.partial(_ret_sample_kernel, H=H)
    yr, ns = pl.pallas_call(
        kern,
        grid=(DB,),
        in_specs=[
            pl.BlockSpec((1, 1, H, DK, DV), lambda b: (layer, b, 0, 0, 0)),
            pl.BlockSpec((1, DK, H), lambda b: (b, 0, 0)),
            pl.BlockSpec((1, DK, H), lambda b: (b, 0, 0)),
            pl.BlockSpec((1, H, DV), lambda b: (b, 0, 0)),
            pl.BlockSpec((1, H, DV), lambda b: (b, 0, 0)),
            pl.BlockSpec((H, 1, DV), lambda b: (0, 0, 0)),
        ],
        out_specs=[
            pl.BlockSpec((1, H, DV), lambda b: (b, 0, 0)),
            pl.BlockSpec((1, 1, H, DK, DV), lambda b: (0, b, 0, 0, 0)),
        ],
        out_shape=[jax.ShapeDtypeStruct((DB, H, DV), F32),
                   jax.ShapeDtypeStruct((1, DB, H, DK, DV), state_ret.dtype)],
        compiler_params=_cparams("parallel"),
        name="retention_sample",
    )(state_ret, qt, kt, rv.reshape(DB, H, DV), rg.reshape(DB, H, DV), g1)
    return yr.reshape(DB, H * DV), ns[0]


def _sortable_key(x):
    bits = pltpu.bitcast(x, I32)
    return bits ^ ((bits >> 31) & 0x7FFFFFFF)


def _topk_select(keys_ref, nvis, ksel, col_bits, half_ref=None):
    _, R, CW = keys_ref.shape
    ksel_f = float(ksel)

    def lane_fold(f, acc):
        for g in range(CW // LANES):
            acc = acc + f[:, g * LANES:(g + 1) * LANES]
        return acc

    def count(pred):
        body = lambda c, acc: lane_fold(jnp.where(pred(keys_ref[c], c), 1.0, 0.0), acc)
        return jnp.sum(lax.fori_loop(0, nvis, body, jnp.zeros((R, LANES), F32)), axis=1, keepdims=True)

    if half_ref is None:
        def value_bit(bi, u):
            cand_u = u | lax.shift_left(jnp.int32(1), 31 - bi)
            cand = cand_u ^ INT_MIN
            cnt = count(lambda key, c: key >= cand)
            return jnp.where(cnt >= ksel_f, cand_u, u)

        T = lax.fori_loop(0, 32, value_bit, jnp.zeros((R, 1), I32)) ^ INT_MIN
    else:
        I16 = jnp.int16
        HALF = 1 << 15

        def row16(v):
            return jnp.concatenate([jnp.broadcast_to(v, (R, LANES)).astype(I16)] * (CW // LANES), axis=1)

        def count16(pred):
            one, zero = jnp.ones((R, CW), I16), jnp.zeros((R, CW), I16)
            body = lambda c, acc: lane_fold(jnp.where(pred(half_ref[c]), one, zero), acc)
            acc = lax.fori_loop(0, nvis, body, jnp.zeros((R, LANES), I16))
            return jnp.sum(acc.astype(F32), axis=1, keepdims=True)

        def search16(target):
            def bit(bi, u):
                cand_u = u | lax.shift_left(jnp.int32(1), 15 - bi)
                cand = row16(cand_u - HALF)
                cnt = count16(lambda half: half >= cand)
                return jnp.where(cnt >= target, cand_u, u)
            return lax.fori_loop(0, 16, bit, jnp.zeros((R, 1), I32))

        def fill_high(c, carry):
            half_ref[c] = (keys_ref[c] >> 16).astype(I16)
            return carry

        lax.fori_loop(0, nvis, fill_high, 0)
        hi = search16(ksel_f) - HALF
        hi16 = row16(hi)
        above = count16(lambda half: half > hi16)

        def fill_low(c, carry):
            low = ((keys_ref[c] & 0xFFFF) - HALF).astype(I16)
            half_ref[c] = jnp.where(half_ref[c] == hi16, low, jnp.full((R, CW), -HALF, I16))
            return carry

        lax.fori_loop(0, nvis, fill_low, 0)
        T = lax.shift_left(hi, 16) | search16(ksel_f - above)

    need = ksel_f - count(lambda key, c: key > T)
    if col_bits is None:
        return T, need
    ties = count(lambda key, c: key == T)

    def col_of(c):
        return c * CW + lax.broadcasted_iota(I32, (R, CW), 1)

    def col_search():
        def col_bit(bi, x):
            cand = x | lax.shift_left(jnp.int32(1), col_bits - 1 - bi)
            cnt = count(lambda key, c: (key == T) & (col_of(c) < cand))
            return jnp.where(cnt < need, cand, x)
        return lax.fori_loop(0, col_bits, col_bit, jnp.zeros((R, 1), I32))

    contested = jnp.max(jnp.where((ties > need) & (T != INT_MIN), 1.0, 0.0)) > 0.0
    J = lax.cond(contested, col_search, lambda: jnp.full((R, 1), 2 ** 30, I32))
    J = jnp.where(T == INT_MIN, -1, J)
    return T, J


def _dsa_select_kernel(iq_ref, iw_ref, ik_ref, bias_ref, keys_ref, half_ref, *, ksel):
    NC, QB, CW = keys_ref.shape
    i = pl.program_id(0)
    nvis = ((i + 1) * QB + CW - 1) // CW
    w = iw_ref[...] * (IDX_HEADS ** -0.5 * IDX_DIM ** -0.5)
    rowpos = i * QB + lax.broadcasted_iota(I32, (QB, 1), 0)

    def col_of(c):
        return c * CW + lax.broadcasted_iota(I32, (QB, CW), 1)

    def score_chunk(c, carry):
        kt = ik_ref[c]
        col = c * CW + lax.broadcasted_iota(I32, (Q_BLOCK, CW), 1)
        for g in range(QB // Q_BLOCK):
            rows = slice(g * Q_BLOCK, (g + 1) * Q_BLOCK)
            s = _dot(iq_ref[0, g], kt)
            acc = jnp.zeros((Q_BLOCK, CW), F32)
            for h in range(IDX_HEADS):
                acc = acc + jnp.maximum(s[h * Q_BLOCK:(h + 1) * Q_BLOCK], 0.0) * w[rows, h:h + 1]
            keys_ref[c, rows, :] = jnp.where(col <= rowpos[rows], _sortable_key(acc), INT_MIN)
        return carry

    lax.fori_loop(0, nvis, score_chunk, 0)
    T, need = _topk_select(keys_ref, nvis, ksel, None, half_ref)

    tri = (lax.broadcasted_iota(I32, (CW, CW), 0) <= lax.broadcasted_iota(I32, (CW, CW), 1)).astype(BF16)

    def emit(c, before):
        key = keys_ref[c]
        tie = jnp.where(key == T, 1.0, 0.0)
        rank = _dot(tie.astype(BF16), tri) + before
        sel = ((key > T) | ((key == T) & (rank <= need))) & (col_of(c) <= rowpos)
        bias_ref[0, c] = jnp.where(sel, 0.0, NEG_BIAS).astype(bias_ref.dtype)
        return before + jnp.sum(tie, axis=-1, keepdims=True)

    lax.fori_loop(0, nvis, emit, jnp.zeros((QB, 1), F32))

    def fill(c, carry):
        bias_ref[0, c] = jnp.full((QB, CW), NEG_BIAS, bias_ref.dtype)
        return carry

    lax.fori_loop(nvis, NC, fill, 0)


def _dsa_select_prompt(iq_slabs, iw, ik, ksel, QB, CW):
    S = iw.shape[0]
    Hi, Di = IDX_HEADS, IDX_DIM
    assert Q_BLOCK == LANES and Di == LANES
    NQ, NC, G = S // QB, S // CW, QB // Q_BLOCK
    iq_stacked = iq_slabs.reshape(NQ, G, Hi * Q_BLOCK, Di)
    ik_cols = ik.reshape(NC, CW, Di).transpose(0, 2, 1)
    kern = functools.partial(_dsa_select_kernel, ksel=ksel)
    return pl.pallas_call(
        kern,
        grid=(NQ,),
        in_specs=[
            pl.BlockSpec((1, G, Hi * Q_BLOCK, Di), lambda i: (i, 0, 0, 0)),
            pl.BlockSpec((QB, iw.shape[1]), lambda i: (i, 0)),
            pl.BlockSpec((NC, Di, CW), lambda i: (0, 0, 0)),
        ],
        out_specs=pl.BlockSpec((1, NC, QB, CW), lambda i: (i, 0, 0, 0)),
        out_shape=jax.ShapeDtypeStruct((NQ, NC, QB, CW), BF16),
        scratch_shapes=[pltpu.VMEM((NC, QB, CW), I32), pltpu.VMEM((NC, QB, CW), jnp.int16)],
        compiler_params=_cparams("parallel"),
        name="dsa_select_prompt",
    )(iq_stacked, iw, ik_cols)


def _dsa_attn_kernel(qi_ref, kj_ref, fin_ref, q_ref, kt_ref, v_ref, b_ref, o_ref, m_sc, l_sc, acc_sc, s_sc, a_sc,
                     *, H, DH, TK, SUB):
    step = pl.program_id(0)

    @pl.when(kj_ref[step] == 0)
    def _():
        m_sc[...] = jnp.full(m_sc.shape, NEG_BIAS, F32)
        l_sc[...] = jnp.zeros_like(l_sc)
        acc_sc[...] = jnp.zeros_like(acc_sc)

    ones = jnp.ones((SUB, LANES), BF16)
    for t in range(TK // SUB):
        keys = slice(t * SUB, (t + 1) * SUB)
        for h in range(H):
            cols = slice(h * DH, (h + 1) * DH)
            s_sc[h] = _dot(q_ref[:, cols], kt_ref[cols, keys]).astype(BF16) + b_ref[0, 0, :, keys]
        for h in range(H):
            s = s_sc[h]
            m_prev = m_sc[h]
            m_new = jnp.maximum(m_prev, jnp.max(s, axis=-1, keepdims=True).astype(F32))
            a_sc[h] = jnp.exp2(m_prev - m_new)
            s_sc[h] = jnp.exp2(s - jnp.concatenate([m_new.astype(BF16)] * (SUB // LANES), axis=1))
            m_sc[h] = m_new
        for h in range(H):
            cols = slice(h * DH, (h + 1) * DH)
            pv = _dot(s_sc[h], jnp.concatenate([v_ref[keys, cols], ones], axis=1))
            alpha = a_sc[h]
            l_sc[h] = alpha * l_sc[h] + pv[:, DH:]
            acc_sc[:, cols] = alpha * acc_sc[:, cols] + pv[:, :DH]

    @pl.when(fin_ref[step] == 1)
    def _():
        for h in range(H):
            cols = slice(h * DH, (h + 1) * DH)
            o_ref[:, cols] = (acc_sc[:, cols] / l_sc[h]).astype(o_ref.dtype)


def _dsa_attn_prompt(aq, ak_t, av, bias, QB, TK):
    S, W = aq.shape
    H, DH = ATT_HEADS, ATT_DH
    assert DH == LANES
    bq = bias.shape[2] // QB
    pairs = [(i, j) for i in range(S // QB) for j in range(((i + 1) * QB - 1) // TK + 1)]
    qi = jnp.asarray([p[0] for p in pairs], I32)
    kj = jnp.asarray([p[1] for p in pairs], I32)
    fin = jnp.asarray([int(p[1] == ((p[0] + 1) * QB - 1) // TK) for p in pairs], I32)
    kern = functools.partial(_dsa_attn_kernel, H=H, DH=DH, TK=TK, SUB=_pick(TK, (512, 256, 128)))
    return pl.pallas_call(
        kern,
        grid_spec=pltpu.PrefetchScalarGridSpec(
            num_scalar_prefetch=3,
            grid=(len(pairs),),
            in_specs=[
                pl.BlockSpec((QB, W), lambda s, qi, kj, fin: (qi[s], 0)),
                pl.BlockSpec((W, TK), lambda s, qi, kj, fin: (0, kj[s])),
                pl.BlockSpec((TK, W), lambda s, qi, kj, fin: (kj[s], 0)),
                pl.BlockSpec((1, 1, QB, TK), lambda s, qi, kj, fin: (qi[s] // bq, kj[s], qi[s] % bq, 0)),
            ],
            out_specs=pl.BlockSpec((QB, W), lambda s, qi, kj, fin: (qi[s], 0)),
            scratch_shapes=[pltpu.VMEM((H, QB, LANES), F32), pltpu.VMEM((H, QB, LANES), F32),
                            pltpu.VMEM((QB, W), F32), pltpu.VMEM((H, QB, _pick(TK, (512, 256, 128))), BF16),
                            pltpu.VMEM((H, QB, LANES), F32)],
        ),
        out_shape=jax.ShapeDtypeStruct((S, W), BF16),
        compiler_params=_cparams("arbitrary"),
        name="dsa_attn_prompt",
    )(qi, kj, fin, aq, ak_t, av, bias)


def _idx_sample_kernel(pt_ref, iq_ref, w_ref, *refs):
    page_refs, o_ref = refs[:-1], refs[-1]
    iq = iq_ref[0].astype(BF16)
    w = w_ref[0]
    rows = []
    for page_ref in page_refs:
        s = _dot_nt(iq, page_ref[0, 0].astype(BF16))
        rows.append(jnp.sum(jnp.maximum(s, 0.0) * w, axis=0, keepdims=True))
    o_ref[0, 0] = jnp.concatenate(rows, axis=1)


def _idx_sample(cache_idx_k, layer, page_table, iq, iw):
    DB, NP = page_table.shape
    P, Di = cache_idx_k.shape[2:]
    Hi = IDX_HEADS
    G = _pick(NP, (32, 16, 8, 4, 2, 1))
    w = jnp.broadcast_to((iw * (IDX_HEADS ** -0.5 * IDX_DIM ** -0.5))[:, :, None], (DB, Hi, P))

    def page_spec(r):
        return pl.BlockSpec((1, 1, P, Di), lambda b, g, pt: (layer, pt[b, g * G + r], 0, 0))

    out = pl.pallas_call(
        _idx_sample_kernel,
        grid_spec=pltpu.PrefetchScalarGridSpec(
            num_scalar_prefetch=1,
            grid=(DB, NP // G),
            in_specs=[
                pl.BlockSpec((1, Hi, Di), lambda b, g, pt: (b, 0, 0)),
                pl.BlockSpec((1, Hi, P), lambda b, g, pt: (b, 0, 0)),
            ] + [page_spec(r) for r in range(G)],
            out_specs=pl.BlockSpec((1, 1, 1, G * P), lambda b, g, pt: (b, g, 0, 0)),
        ),
        out_shape=jax.ShapeDtypeStruct((DB, NP // G, 1, G * P), F32),
        compiler_params=_cparams("parallel", "arbitrary"),
        name="dsa_index_sample",
    )(page_table, iq.reshape(DB, Hi, Di), w, *([cache_idx_k] * G))
    return out.reshape(DB, NP * P)


def _select_sample_kernel(sc_ref, iq_ref, ikn_ref, iw_ref, idx_ref, keys_ref, cnt_ref, *, ksel, past_len,
                          col_bits):
    NC, R, CW = keys_ref.shape
    NP = NC - 1
    w = iw_ref[...] * (IDX_HEADS ** -0.5 * IDX_DIM ** -0.5)
    ikn = ikn_ref[...]
    new = jnp.zeros((R, 1), F32)
    for h in range(IDX_HEADS):
        s = jnp.sum(iq_ref[:, h * IDX_DIM:(h + 1) * IDX_DIM] * ikn, axis=-1, keepdims=True)
        new = new + jnp.maximum(s, 0.0) * w[:, h:h + 1]

    def col_of(c):
        return c * CW + lax.broadcasted_iota(I32, (R, CW), 1)

    def load(c, carry):
        keys_ref[c] = _sortable_key(sc_ref[c] + 0.0)
        return carry

    lax.fori_loop(0, NP, load, 0)
    keys_ref[NP] = jnp.where(col_of(NP) <= past_len, _sortable_key(jnp.broadcast_to(new, (R, CW))), INT_MIN)
    T, J = _topk_select(keys_ref, NC, ksel, col_bits)

    tri = (lax.broadcasted_iota(I32, (CW, CW), 0) <= lax.broadcasted_iota(I32, (CW, CW), 1)).astype(BF16)

    def prefix(c, before):
        key = keys_ref[c]
        col = col_of(c)
        sel = ((key > T) | ((key == T) & (col <= J))) & (col <= past_len)
        f = jnp.where(sel, 1.0, 0.0)
        cnt_ref[c] = _dot(f.astype(BF16), tri) + before
        return before + jnp.sum(f, axis=-1, keepdims=True)

    lax.fori_loop(0, NC, prefix, jnp.zeros((R, 1), F32))

    jcol = lax.broadcasted_iota(I32, (ksel, 1), 0).astype(F32)
    lane = lax.broadcasted_iota(I32, (ksel, LANES), 1)

    def one_row(r, out):
        def chunk(c, acc):
            return acc + jnp.where(cnt_ref[c, pl.ds(r, 1), :] <= jcol, 1.0, 0.0)
        acc = lax.fori_loop(0, NC, chunk, jnp.zeros((ksel, CW), F32))
        return jnp.where(lane == r, jnp.sum(acc, axis=-1, keepdims=True), out)

    idx_ref[...] = lax.fori_loop(0, R, one_row, jnp.zeros((ksel, LANES), F32)).astype(I32)


def _select_sample(scores, iq, ik_new, iw, ksel):
    DB = scores.shape[0]
    P = LANES
    NP = scores.shape[1] // P
    past_len = NP * P
    assert DB <= LANES
    sc = scores.reshape(DB, NP, P).transpose(1, 0, 2)
    kern = functools.partial(_select_sample_kernel, ksel=ksel, past_len=past_len,
                             col_bits=int(np.ceil(np.log2(past_len + P))))
    idx_t = pl.pallas_call(
        kern,
        out_shape=jax.ShapeDtypeStruct((ksel, LANES), I32),
        scratch_shapes=[pltpu.VMEM((NP + 1, DB, P), I32), pltpu.VMEM((NP + 1, DB, P), F32)],
        compiler_params=pltpu.CompilerParams(vmem_limit_bytes=VMEM_LIMIT_BYTES),
        name="dsa_select_sample",
    )(sc, iq, ik_new, iw)
    return idx_t[:, :DB].T


def _attn_sample_kernel(idx_ref, pt_ref, q_ref, kn_ref, vn_ref, ck_hbm, cv_hbm, o_ref, kbuf, vbuf, sem,
                        m_sc, l_sc, acc_sc, *, layer, G, ksel, past_len, page):
    b = pl.program_id(0)
    slot = b % 2
    scale = q_ref.shape[-1] ** -0.5

    def row_copies(seq, r, sl):
        pos = jnp.minimum(idx_ref[seq * ksel + r], past_len - 1)
        src = (layer, pt_ref[seq, pos // page], pos % page)
        return (pltpu.make_async_copy(ck_hbm.at[src], kbuf.at[sl, r], sem.at[sl, 0]),
                pltpu.make_async_copy(cv_hbm.at[src], vbuf.at[sl, r], sem.at[sl, 1]))

    def start_rows(seq, sl):
        def body(r, carry):
            for cp in row_copies(seq, r, sl):
                cp.start()
            return carry
        lax.fori_loop(0, ksel, body, 0)

    @pl.when(b == 0)
    def _():
        start_rows(0, 0)

    @pl.when(b + 1 < pl.num_programs(0))
    def _():
        start_rows(b + 1, 1 - slot)

    def wait_row(r, carry):
        for cp in row_copies(b, r, slot):
            cp.wait()
        return carry

    lax.fori_loop(0, ksel, wait_row, 0)

    m_sc[...] = jnp.full(m_sc.shape, NEG_BIAS, F32)
    l_sc[...] = jnp.zeros_like(l_sc)
    acc_sc[...] = jnp.zeros_like(acc_sc)
    q = q_ref[0]

    def slot_bias(r):
        pos = idx_ref[b * ksel + r]
        return jnp.full((1, 1, 1), jnp.where(pos >= past_len, NEG_BIAS, 0.0), F32)

    def update(kk, vv, bias3):
        s = jnp.sum(kk * q[None], axis=-1, keepdims=True) * scale + bias3
        m_prev = m_sc[...]
        m_new = jnp.maximum(m_prev, jnp.max(s, axis=0))
        alpha = jnp.exp(m_prev - m_new)
        pr = jnp.exp(s - m_new[None])
        l_sc[...] = alpha * l_sc[...] + jnp.sum(pr, axis=0)
        acc_sc[...] = alpha * acc_sc[...] + jnp.sum(pr * vv, axis=0)
        m_sc[...] = m_new

    for c in range(ksel // G):
        rows = pl.ds(c * G, G)
        update(kbuf[slot, rows], vbuf[slot, rows],
               jnp.concatenate([slot_bias(c * G + r) for r in range(G)], axis=0))

    last = idx_ref[b * ksel + ksel - 1]
    update(kn_ref[...], vn_ref[...], jnp.full((1, 1, 1), jnp.where(last >= past_len, 0.0, NEG_BIAS), F32))
    o_ref[0] = acc_sc[...] / l_sc[...]


def _attn_sample(cache_k, cache_v, layer, page_table, aq, ak_new, av_new, idx):
    DB, NP = page_table.shape
    P, H, DH = cache_k.shape[2:]
    ksel = idx.shape[1]
    vec_spec = pl.BlockSpec((1, H, DH), lambda b, idx_ref, pt: (b, 0, 0))
    kern = functools.partial(_attn_sample_kernel, layer=layer, G=_pick(ksel, (32, 16, 8, 4, 2, 1)), ksel=ksel,
                             past_len=NP * P, page=P)
    out = pl.pallas_call(
        kern,
        grid_spec=pltpu.PrefetchScalarGridSpec(
            num_scalar_prefetch=2,
            grid=(DB,),
            in_specs=[vec_spec, vec_spec, vec_spec, pl.BlockSpec(memory_space=pl.ANY),
                      pl.BlockSpec(memory_space=pl.ANY)],
            out_specs=vec_spec,
            scratch_shapes=[pltpu.VMEM((2, ksel, H, DH), F32), pltpu.VMEM((2, ksel, H, DH), F32),
                            pltpu.SemaphoreType.DMA((2, 2)),
                            pltpu.VMEM((H, 1), F32), pltpu.VMEM((H, 1), F32), pltpu.VMEM((H, DH), F32)],
        ),
        out_shape=jax.ShapeDtypeStruct((DB, H, DH), F32),
        compiler_params=_cparams("arbitrary"),
        name="dsa_attn_sample",
    )(idx.reshape(-1), page_table, aq.reshape(DB, H, DH), ak_new.reshape(DB, H, DH), av_new.reshape(DB, H, DH),
      cache_k, cache_v)
    return out.reshape(DB, H * DH)


def _rotary_tables(pos):
    d = RET_DK
    inv_freq = 1.0 / (ROPE_BASE ** jnp.linspace(0.0, 1.0, d // 2, dtype=F32))
    ang = pos.astype(F32)[:, None] * inv_freq[None, :]
    cos = jnp.repeat(jnp.cos(ang), 2, axis=1)
    sin = jnp.sin(ang)
    sin_signed = jnp.stack([-sin, sin], axis=-1).reshape(pos.shape[0], d)
    return cos, sin_signed


def _in_proj_columns(w_in):
    rqk, rv_w = RET_HEADS * RET_DK, RET_HEADS * RET_DV
    aw, iqw = ATT_HEADS * ATT_DH, IDX_HEADS * IDX_DIM
    d_model = w_in.shape[1]
    names = ("rq", "rk", "rv", "rg", "aq", "ak", "av", "iq", "ik", "iw", "ga", "gb")
    sizes = (rqk, rqk, rv_w, rv_w, aw, aw, aw, iqw, IDX_DIM, IDX_HEADS, d_model, d_model)
    offs = np.concatenate([[0], np.cumsum(sizes)]).tolist()
    assert offs[-1] == w_in.shape[2]
    return {n: (offs[t], sizes[t]) for t, n in enumerate(names)}


def _mix_inputs(hp, hs, w_in_t, layer, cols, qg, kg, rot_p, rot_s, q_scale):
    tm = _pick(hp.shape[0], ROW_TILES)
    P, Sm = {}, {}

    def proj(name, epi_p, ex_p, out_p, epi_s, ex_s, out_s, tn_prefs=(512, 256, 128), n=None):
        col0 = cols[name][0]
        n = n or cols[name][1]
        outs = _mm([hp], [(0, w_in_t, layer, col0, n, True)], ex_p, epi_p, [d for _, d in out_p], tm=tm,
                   tn=_pick(n, tn_prefs), name="proj_" + name, rider=([hs], ex_s, epi_s, [d for _, d in out_s]))
        for (key, _), o in zip(out_p, outs):
            P[key] = o
        for (key, _), o in zip(out_s, outs[len(out_p):]):
            Sm[key] = o

    def same(name, dt_p, **kw):
        proj(name, _epi_identity, [], [(name, dt_p)], _epi_identity, [], [(name, F32)], **kw)

    rot_p = [(t, "rows") for t in rot_p]
    rot_s = [(t, "rows") for t in rot_s]
    for name, scale in (("rq", 1.0), ("rk", RET_DK ** -0.5)):
        epi = functools.partial(_epi_rotary, scale)
        proj(name, epi, rot_p, [(name, BF16)], epi, rot_s, [(name, F32)])
    same("rv", BF16)
    same("rg", F32)
    norm1, norm2 = functools.partial(_epi_head_norm, 1), functools.partial(_epi_head_norm, 2)
    proj("aq", norm1, [(qg.reshape(1, -1) * q_scale, "full")], [("aq", BF16)],
         norm1, [(qg.reshape(1, -1), "full")], [("aq", F32)])
    kgain = [(kg.reshape(1, -1), "full")]
    proj("ak", norm2, kgain, [("ak", F32), ("ak16t", (BF16, "T"))], norm1, kgain, [("ak", F32)])
    proj("av", _epi_two_copies, [], [("av", F32), ("av16", BF16)], _epi_identity, [], [("av", F32)])
    same("iq", (BF16, "slabs"))
    same("ik", F32, tn_prefs=(2 * LANES,), n=2 * LANES)
    same("ga", F32)
    same("gb", F32)
    for d in (P, Sm):
        ikw = d["ik"]
        d["ik"] = ikw[:, :IDX_DIM]
        d["iw"] = ikw[:, IDX_DIM:IDX_DIM + IDX_HEADS]
    return P, Sm


def _merge_ffn(xp, xs, mp, ms, layer, w_ret_o, w_att_o, w_out, g_ffn, w_gate, w_up, wd):
    M, D = xp.shape
    F = w_gate.shape[2]
    tm = _pick(M, ROW_TILES)
    gates = lambda t: [(t[2], "tile"), (t[3], "tile")]
    m_p, m_s = _mm(list(mp[:2]), [(0, w_ret_o, layer, 0, D, False), (1, w_att_o, layer, 0, D, False)], gates(mp),
                   _epi_merge, [BF16], tm=_pick(M, ROW_TILES[1:]), tn=_pick(D, (512, 256, 128)), name="merge",
                   rider=(list(ms[:2]), gates(ms), _epi_merge, [BF16]))
    x1p, x1s = _mm([m_p], [(0, w_out, layer, 0, D, False)], [(xp, "tile")], _epi_residual, [F32], tm=tm,
                   tn=_pick(D, (512, 256, 128)), name="out_proj",
                   rider=([m_s], [(xs, "tile")], _epi_residual, [F32]))
    hfp, hfs = _rmsnorm(x1p, g_ffn), _rmsnorm(x1s, g_ffn)
    up, us = _mm([hfp], [(0, w_gate, layer, 0, F, False), (0, w_up, layer, 0, F, False)], [], _epi_swiglu, [BF16],
                 tm=tm, tn=_pick(F, (256, 128)), name="ffn_up", rider=([hfs], [], _epi_swiglu, [BF16]))
    down = lambda u, x1: _mm([u], [(0, wd, None, 0, D, False)], [(x1, "tile")], _epi_residual, [F32],
                             tm=_pick(u.shape[0], ROW_TILES[1:]), tn=_pick(D, (256, 128)), name="ffn_down",
                             hold_b=False)[0]
    return down(up, x1p), down(us, x1s)


def kernel(x_prompt, x_sample, cache_k, cache_v, cache_idx_k, state_ret, page_table, norm_mix_g, w_in,
           q_norm_g, k_norm_g, w_ret_o, w_att_o, w_out, norm_ffn_g, w_ffn_gate, w_ffn_up, w_ffn_down):
    depth = w_in.shape[0]
    B, S, D = x_prompt.shape
    DB, T, _ = x_sample.shape
    assert B == 1 and T == 1
    page_size = cache_k.shape[2]
    past_len = page_table.shape[1] * page_size

    log_gamma = jnp.log1p(-jnp.exp2(-5.0 - jnp.arange(RET_HEADS, dtype=F32)))
    ret_tabs = _retention_tables(log_gamma)
    cos_p, sin_p = _rotary_tables(jnp.arange(S, dtype=I32))
    cos_s, sin_s = _rotary_tables(jnp.full((DB,), past_len, I32))
    ksel_p = min(TOPK_MAX, S // 4)
    ksel_s = min(TOPK_MAX, (past_len + T) // 4)
    cw = _pick(S, (512, 256, 128))
    qb = _pick(S, (2 * Q_BLOCK, Q_BLOCK))
    cols = _in_proj_columns(w_in)
    w_in_t = jnp.swapaxes(w_in, 1, 2)

    xp = x_prompt.reshape(S, D)
    xs = x_sample.reshape(DB, D)
    kp, vp, ikp, stp, ksm, vsm, iks, sts = [], [], [], [], [], [], [], []
    for layer in range(depth):
        wd = w_ffn_down[layer].astype(BF16)
        dense = (layer, w_ret_o, w_att_o, w_out, norm_ffn_g[layer], w_ffn_gate, w_ffn_up, wd)

        p, s = _mix_inputs(_rmsnorm(xp, norm_mix_g[layer]), _rmsnorm(xs, norm_mix_g[layer]), w_in_t, layer, cols,
                           q_norm_g[layer], k_norm_g[layer], (cos_p, sin_p), (cos_s, sin_s),
                           ATT_DH ** -0.5 * float(np.log2(np.e)))

        yr_p, st_p = _retention_prompt(p["rq"], p["rk"], p["rv"], p["rg"], ret_tabs)
        bias = _dsa_select_prompt(p["iq"], p["iw"], p["ik"].astype(BF16), ksel_p, qb, cw)
        ya_p = _dsa_attn_prompt(p["aq"], p["ak16t"], p["av16"], bias, qb, cw)

        yr_s, st_s = _retention_sample(state_ret, layer, s["rq"], s["rk"], s["rv"], s["rg"], log_gamma)
        scores = _idx_sample(cache_idx_k, layer, page_table, s["iq"], s["iw"])
        sel_idx = _select_sample(scores, s["iq"], s["ik"], s["iw"], ksel_s)
        ya_s = _attn_sample(cache_k, cache_v, layer, page_table, s["aq"], s["ak"], s["av"], sel_idx)

        xp, xs = _merge_ffn(xp, xs, (yr_p, ya_p, p["ga"], p["gb"]),
                            (yr_s.astype(BF16), ya_s.astype(BF16), s["ga"], s["gb"]), *dense)
        kp.append(p["ak"].reshape(B, S, ATT_HEADS, ATT_DH))
        vp.append(p["av"].reshape(B, S, ATT_HEADS, ATT_DH))
        ikp.append(p["ik"].reshape(B, S, IDX_DIM))
        stp.append(st_p[None].astype(state_ret.dtype))
        ksm.append(s["ak"].reshape(DB, T, ATT_HEADS, ATT_DH))
        vsm.append(s["av"].reshape(DB, T, ATT_HEADS, ATT_DH))
        iks.append(s["ik"].reshape(DB, T, IDX_DIM))
        sts.append(st_s)
    return (xp.reshape(B, S, D), xs.reshape(DB, T, D), jnp.stack(kp), jnp.stack(vp), jnp.stack(ikp),
            jnp.stack(stp), jnp.stack(ksm), jnp.stack(vsm), jnp.stack(iks), jnp.stack(sts))
```
